```python
import jax, jax.numpy as jnp
from jax import lax
import numpy as np

D_MODEL = 4096
BATCH = 4
SEQ = 2048
DEPTH = 1
DEC_BATCH = 128
DEC_SEQ = 4
PAST_LEN = 2048
PAGE_SIZE = 128

HEAD_DIM = 128
SB_HEADS = D_MODEL // (2 * HEAD_DIM)
NSA_HEADS = D_MODEL // (2 * HEAD_DIM)
NSA_KV_HEADS = 4
NSA_GROUP = NSA_HEADS // NSA_KV_HEADS
CMP_LEN = 32
CMP_STRIDE = 16
CMP_HID = 2 * HEAD_DIM
SLC_LEN = 64
N_SEL = 16
WINDOW = 512
ROPE_THETA = 500000.0
ROPE_DIM = HEAD_DIM // 4
D_FF = -(-8 * D_MODEL // (3 * 256)) * 256
Q_BLOCK = 128
RMS_EPS = 1e-6
NEG = -1e30

SB_W = SB_HEADS * HEAD_DIM
NSA_W = NSA_HEADS * HEAD_DIM
KV_W = NSA_KV_HEADS * HEAD_DIM
IN_WIDTHS = (SB_W, SB_W, SB_W, NSA_W, KV_W, KV_W, KV_W, KV_W, KV_W, KV_W, 3 * NSA_HEADS, D_MODEL, D_MODEL)
D_IN = sum(IN_WIDTHS)

kernel_name = 'stickbreak_nsa_hybrid_step'


def rms_norm(x, g):
    xf = x.astype(jnp.float32)
    y = xf * lax.rsqrt(jnp.mean(xf * xf, axis=-1, keepdims=True) + RMS_EPS)
    return (y * g.astype(jnp.float32)).astype(x.dtype)


def partial_rope(x, pos):
    half = ROPE_DIM // 2
    inv = jnp.float32(ROPE_THETA) ** (-(jnp.arange(half, dtype=jnp.float32) * 2.0 / ROPE_DIM))
    ang = pos.astype(jnp.float32)[:, None] * inv[None, :]
    cos = jnp.cos(ang)[None, :, None, :]
    sin = jnp.sin(ang)[None, :, None, :]
    xf = x.astype(jnp.float32)
    x1, x2, rest = xf[..., :half], xf[..., half:ROPE_DIM], xf[..., ROPE_DIM:]
    out = jnp.concatenate([x1 * cos - x2 * sin, x2 * cos + x1 * sin, rest], axis=-1)
    return out.astype(x.dtype)


def masked_softmax(s, mask):
    p = jax.nn.softmax(jnp.where(mask, s, NEG), axis=-1)
    return jnp.where(mask, p, 0.0)


def sb_attend(q, q_pos, k, v, k_pos):
    z = jnp.einsum('bqhd,bkhd->bhqk', q, k).astype(jnp.float32) * (HEAD_DIM ** -0.5)
    mask = (k_pos[None, :] < q_pos[:, None])[None, None]
    log1m = jnp.where(mask, jax.nn.log_sigmoid(-z), 0.0)
    later = lax.cumsum(log1m, axis=3, reverse=True) - log1m
    a = jnp.where(mask, jnp.exp(jax.nn.log_sigmoid(z) + later), 0.0)
    return jnp.einsum('bhqk,bkhd->bqhd', a.astype(v.dtype), v)


def compress(t, w1, w2, pe):
    b, l, h, d = t.shape
    r = CMP_LEN // CMP_STRIDE
    n_chunk = l // CMP_STRIDE
    n_cmp = n_chunk - r + 1
    chunks = t[:, :n_chunk * CMP_STRIDE].reshape(b, n_chunk, CMP_STRIDE, h, d)
    u = jnp.einsum('bcshd,msdn->bmchn', chunks, w1.reshape(r, CMP_STRIDE, d, CMP_HID))
    hid = u[:, 0, :n_cmp]
    for m in range(1, r):
        hid = hid + u[:, m, m:m + n_cmp]
    hid = hid + jnp.einsum('ld,ldn->n', pe, w1)
    return jnp.einsum('bchn,nd->bchd', jax.nn.silu(hid), w2)


def selection_blocks(t):
    b, l, h, d = t.shape
    n = -(-l // SLC_LEN)
    t = jnp.pad(t, ((0, 0), (0, n * SLC_LEN - l), (0, 0), (0, 0)))
    return t.reshape(b, n, SLC_LEN, h, d).transpose(0, 3, 1, 2, 4)


def cmp_to_slc_matrix(n_cmp, n_slc):
    i = np.arange(n_cmp)[:, None]
    j = np.arange(n_slc)[None, :]
    lo = np.maximum(i * CMP_STRIDE, j * SLC_LEN)
    hi = np.minimum(i * CMP_STRIDE + CMP_LEN, j * SLC_LEN + SLC_LEN)
    return jnp.asarray((np.clip(hi - lo, 0, None) / CMP_LEN).astype(np.float32))


def nsa_attend(q, q_pos, gates, cmp_k, cmp_v, slc_kb, slc_vb, win_k, win_v, win_pos):
    b, tq = q.shape[:2]
    scale = HEAD_DIM ** -0.5
    n_cmp, n_slc = cmp_k.shape[1], slc_kb.shape[2]
    s_c = jnp.einsum('bqhgd,bchd->bqhgc', q, cmp_k).astype(jnp.float32) * scale
    cmp_end = jnp.arange(n_cmp) * CMP_STRIDE + (CMP_LEN - 1)
    m_c = (cmp_end[None, :] <= q_pos[:, None])[None, :, None, None, :]
    p_c = masked_softmax(s_c, m_c)
    o_c = jnp.einsum('bqhgc,bchd->bqhgd', p_c.astype(cmp_v.dtype), cmp_v)
    imp = jnp.einsum('bqhc,cs->bqhs', p_c.sum(axis=3), cmp_to_slc_matrix(n_cmp, n_slc))
    blk = jnp.arange(n_slc)[None, :]
    cur = (q_pos // SLC_LEN)[:, None]
    valid = (blk * SLC_LEN <= q_pos[:, None])[None, :, None, :]
    forced = ((blk == 0) | (blk == cur) | (blk == cur - 1))[None, :, None, :]
    score = jnp.where(valid, jnp.where(forced, jnp.inf, imp), -jnp.inf)
    top_score, sel = lax.top_k(score, min(N_SEL, n_slc))
    n_sel = sel.shape[-1]
    b_ix = jnp.arange(b)[:, None, None, None]
    h_ix = jnp.arange(NSA_KV_HEADS)[None, None, :, None]
    k_g = slc_kb[b_ix, h_ix, sel]
    v_g = slc_vb[b_ix, h_ix, sel]
    s_s = jnp.einsum('bqhgd,bqhnld->bqhgnl', q, k_g).astype(jnp.float32) * scale
    key_pos = sel[..., None] * SLC_LEN + jnp.arange(SLC_LEN)
    m_s = (top_score > -jnp.inf)[..., None] & (key_pos <= q_pos[None, :, None, None, None])
    flat = n_sel * SLC_LEN
    p_s = masked_softmax(s_s.reshape(b, tq, NSA_KV_HEADS, NSA_GROUP, flat),
                         m_s.reshape(b, tq, NSA_KV_HEADS, 1, flat)).reshape(s_s.shape)
    o_s = jnp.einsum('bqhgnl,bqhnld->bqhgd', p_s.astype(v_g.dtype), v_g)
    s_w = jnp.einsum('bqhgd,bwhd->bqhgw', q, win_k).astype(jnp.float32) * scale
    dist = q_pos[:, None] - win_pos[None, :]
    m_w = ((dist >= 0) & (dist < WINDOW) & (win_pos[None, :] >= 0))[None, :, None, None, :]
    p_w = masked_softmax(s_w, m_w)
    o_w = jnp.einsum('bqhgw,bwhd->bqhgd', p_w.astype(win_v.dtype), win_v)
    o = gates[..., 0:1] * o_c + gates[..., 1:2] * o_s + gates[..., 2:3] * o_w
    return o.reshape(b, tq, NSA_HEADS, HEAD_DIM)


def mixer_projections(x, pos, lw):
    b, t, _ = x.shape
    xn = rms_norm(x, lw['g_mix'])
    parts = jnp.split(xn @ lw['w_in'], [int(c) for c in np.cumsum(IN_WIDTHS)[:-1]], axis=-1)
    q_sb, k_sb, v_sb, q_n, k_c, v_c, k_s, v_s, k_w, v_w, g_n, g_a, g_b = parts
    heads = lambda a: a.reshape(b, t, -1, HEAD_DIM)
    q_n = partial_rope(heads(q_n), pos).reshape(b, t, NSA_KV_HEADS, NSA_GROUP, HEAD_DIM)
    k_c, k_s, k_w = partial_rope(heads(k_c), pos), partial_rope(heads(k_s), pos), partial_rope(heads(k_w), pos)
    g_n = jax.nn.sigmoid(g_n).reshape(b, t, NSA_KV_HEADS, NSA_GROUP, 3)
    return (heads(q_sb), heads(k_sb), heads(v_sb), q_n, k_c, heads(v_c), k_s, heads(v_s),
            k_w, heads(v_w), g_n, g_a, g_b)


def merge_and_ffn(x, o_sb, o_n, g_a, g_b, lw):
    b, t, _ = x.shape
    y_a = o_sb.reshape(b, t, SB_W) @ lw['w_branch_a']
    y_b = o_n.reshape(b, t, NSA_W) @ lw['w_branch_b']
    h = x + (jax.nn.sigmoid(g_a) * y_a + jax.nn.sigmoid(g_b) * y_b) @ lw['w_out']
    hn = rms_norm(h, lw['g_ffn'])
    return h + (jax.nn.silu(hn @ lw['w_ffn_gate']) * (hn @ lw['w_ffn_up'])) @ lw['w_ffn_down']


def map_query_blocks(fn, n_q):
    out = lax.map(fn, jnp.arange(n_q // Q_BLOCK) * Q_BLOCK)
    out = jnp.moveaxis(out, 0, 1)
    return out.reshape(out.shape[0], n_q, *out.shape[3:])


def gather_pages(pool, page_table):
    g = pool[page_table]
    return g.reshape(g.shape[0], -1, *pool.shape[2:])


def prompt_layer(x, lw):
    b, t, _ = x.shape
    pos = jnp.arange(t)
    q_sb, k_sb, v_sb, q_n, k_c, v_c, k_s, v_s, k_w, v_w, g_n, g_a, g_b = mixer_projections(x, pos, lw)

    def sb_block(start):
        q_blk = lax.dynamic_slice_in_dim(q_sb, start, Q_BLOCK, axis=1)
        return sb_attend(q_blk, start + jnp.arange(Q_BLOCK), k_sb, v_sb, pos)
    o_sb = map_query_blocks(sb_block, t)

    ck = compress(k_c, lw['w_cmp_k1'], lw['w_cmp_k2'], lw['pe_cmp_k'])
    cv = compress(v_c, lw['w_cmp_v1'], lw['w_cmp_v2'], lw['pe_cmp_v'])
    kb, vb = selection_blocks(k_s), selection_blocks(v_s)
    pad = ((0, 0), (WINDOW, 0), (0, 0), (0, 0))
    kw_pad, vw_pad = jnp.pad(k_w, pad), jnp.pad(v_w, pad)

    def nsa_block(start):
        sl = lambda a, n: lax.dynamic_slice_in_dim(a, start, n, axis=1)
        band = WINDOW + Q_BLOCK
        return nsa_attend(sl(q_n, Q_BLOCK), start + jnp.arange(Q_BLOCK), sl(g_n, Q_BLOCK), ck, cv, kb, vb,
                          sl(kw_pad, band), sl(vw_pad, band), start - WINDOW + jnp.arange(band))
    o_n = map_query_blocks(nsa_block, t)

    y = merge_and_ffn(x, o_sb, o_n, g_a, g_b, lw)
    keep = min(WINDOW, t)
    return y, (k_sb, v_sb, k_c, v_c, k_s, v_s, k_w[:, t - keep:], v_w[:, t - keep:])


def sample_layer(x, page_table, caches, lw):
    c_sb_k, c_sb_v, c_cmp_k, c_cmp_v, c_slc_k, c_slc_v, w_buf_k, w_buf_v = caches
    b, t, _ = x.shape
    past_len = page_table.shape[1] * PAGE_SIZE
    pos = past_len + jnp.arange(t)
    q_sb, k_sb, v_sb, q_n, k_c, v_c, k_s, v_s, k_w, v_w, g_n, g_a, g_b = mixer_projections(x, pos, lw)
    full = lambda pool, new: jnp.concatenate([gather_pages(pool, page_table), new], axis=1)

    o_sb = sb_attend(q_sb, pos, full(c_sb_k, k_sb), full(c_sb_v, v_sb), jnp.arange(past_len + t))

    ck = compress(full(c_cmp_k, k_c), lw['w_cmp_k1'], lw['w_cmp_k2'], lw['pe_cmp_k'])
    cv = compress(full(c_cmp_v, v_c), lw['w_cmp_v1'], lw['w_cmp_v2'], lw['pe_cmp_v'])
    kb, vb = selection_blocks(full(c_slc_k, k_s)), selection_blocks(full(c_slc_v, v_s))
    wk = jnp.concatenate([w_buf_k, k_w], axis=1)
    wv = jnp.concatenate([w_buf_v, v_w], axis=1)
    n_buf = w_buf_k.shape[1]
    wpos = past_len - n_buf + jnp.arange(n_buf + t)
    o_n = nsa_attend(q_n, pos, g_n, ck, cv, kb, vb, wk, wv, wpos)

    y = merge_and_ffn(x, o_sb, o_n, g_a, g_b, lw)
    return y, (k_sb, v_sb, k_c, v_c, k_s, v_s, wk[:, t:], wv[:, t:])


def setup_inputs(seed: int = 0) -> dict:
    key = jax.random.key(seed)
    k = jax.random.split(key, 32)
    n_pages = PAST_LEN // PAGE_SIZE
    n_used = DEC_BATCH * n_pages
    n_pool = n_used + n_used // 4
    win_buf = min(WINDOW, PAST_LEN)
    nrm = lambda kk, shape, scale: jax.random.normal(kk, shape, jnp.float32) * scale
    page_table = jax.random.permutation(k[10], n_pool)[:n_used].reshape(DEC_BATCH, n_pages).astype(jnp.int32)
    return {
        'x_prompt': nrm(k[0], (BATCH, SEQ, D_MODEL), 1.0),
        'x_sample': nrm(k[1], (DEC_BATCH, DEC_SEQ, D_MODEL), 1.0),
        'cache_sb_k': nrm(k[2], (DEPTH, n_pool, PAGE_SIZE, SB_HEADS, HEAD_DIM), 1.0),
        'cache_sb_v': nrm(k[3], (DEPTH, n_pool, PAGE_SIZE, SB_HEADS, HEAD_DIM), 1.0),
        'cache_cmp_k': nrm(k[4], (DEPTH, n_pool, PAGE_SIZE, NSA_KV_HEADS, HEAD_DIM), 1.0),
        'cache_cmp_v': nrm(k[5], (DEPTH, n_pool, PAGE_SIZE, NSA_KV_HEADS, HEAD_DIM), 1.0),
        'cache_slc_k': nrm(k[6], (DEPTH, n_pool, PAGE_SIZE, NSA_KV_HEADS, HEAD_DIM), 1.0),
        'cache_slc_v': nrm(k[7], (DEPTH, n_pool, PAGE_SIZE, NSA_KV_HEADS, HEAD_DIM), 1.0),
        'state_win_k': nrm(k[8], (DEPTH, DEC_BATCH, win_buf, NSA_KV_HEADS, HEAD_DIM), 1.0),
        'state_win_v': nrm(k[9], (DEPTH, DEC_BATCH, win_buf, NSA_KV_HEADS, HEAD_DIM), 1.0),
        'page_table': page_table,
        'g_mix': 1.0 + nrm(k[11], (DEPTH, D_MODEL), 0.02),
        'w_in': nrm(k[12], (DEPTH, D_MODEL, D_IN), D_MODEL ** -0.5),
        'w_cmp_k1': nrm(k[13], (DEPTH, CMP_LEN, HEAD_DIM, CMP_HID), (CMP_LEN * HEAD_DIM) ** -0.5),
        'w_cmp_k2': nrm(k[14], (DEPTH, CMP_HID, HEAD_DIM), CMP_HID ** -0.5),
        'pe_cmp_k': nrm(k[15], (DEPTH, CMP_LEN, HEAD_DIM), 1.0),
        'w_cmp_v1': nrm(k[16], (DEPTH, CMP_LEN, HEAD_DIM, CMP_HID), (CMP_LEN * HEAD_DIM) ** -0.5),
        'w_cmp_v2': nrm(k[17], (DEPTH, CMP_HID, HEAD_DIM), CMP_HID ** -0.5),
        'pe_cmp_v': nrm(k[18], (DEPTH, CMP_LEN, HEAD_DIM), 1.0),
        'w_branch_a': nrm(k[19], (DEPTH, SB_W, D_MODEL), SB_W ** -0.5),
        'w_branch_b': nrm(k[20], (DEPTH, NSA_W, D_MODEL), NSA_W ** -0.5),
        'w_out': nrm(k[21], (DEPTH, D_MODEL, D_MODEL), D_MODEL ** -0.5),
        'g_ffn': 1.0 + nrm(k[22], (DEPTH, D_MODEL), 0.02),
        'w_ffn_gate': nrm(k[23], (DEPTH, D_MODEL, D_FF), D_MODEL ** -0.5),
        'w_ffn_up': nrm(k[24], (DEPTH, D_MODEL, D_FF), D_MODEL ** -0.5),
        'w_ffn_down': nrm(k[25], (DEPTH, D_FF, D_MODEL), D_FF ** -0.5),
        'g_final': 1.0 + nrm(k[26], (D_MODEL,), 0.02),
    }


def reference(x_prompt, x_sample, cache_sb_k, cache_sb_v, cache_cmp_k, cache_cmp_v, cache_slc_k, cache_slc_v,
              state_win_k, state_win_v, page_table, g_mix, w_in, w_cmp_k1, w_cmp_k2, pe_cmp_k, w_cmp_v1,
              w_cmp_v2, pe_cmp_v, w_branch_a, w_branch_b, w_out, g_ffn, w_ffn_gate, w_ffn_up, w_ffn_down,
              g_final):
    y_p, y_s = x_prompt, x_sample
    p_states, s_states = [], []
    for l in range(DEPTH):
        lw = dict(g_mix=g_mix[l], w_in=w_in[l], w_cmp_k1=w_cmp_k1[l], w_cmp_k2=w_cmp_k2[l], pe_cmp_k=pe_cmp_k[l],
                  w_cmp_v1=w_cmp_v1[l], w_cmp_v2=w_cmp_v2[l], pe_cmp_v=pe_cmp_v[l], w_branch_a=w_branch_a[l],
                  w_branch_b=w_branch_b[l], w_out=w_out[l], g_ffn=g_ffn[l], w_ffn_gate=w_ffn_gate[l],
                  w_ffn_up=w_ffn_up[l], w_ffn_down=w_ffn_down[l])
        y_p, st_p = prompt_layer(y_p, lw)
        caches = (cache_sb_k[l], cache_sb_v[l], cache_cmp_k[l], cache_cmp_v[l], cache_slc_k[l], cache_slc_v[l],
                  state_win_k[l], state_win_v[l])
        y_s, st_s = sample_layer(y_s, page_table, caches, lw)
        p_states.append(st_p)
        s_states.append(st_s)
    p_sb_k, p_sb_v, p_cmp_k, p_cmp_v, p_slc_k, p_slc_v, p_win_k, p_win_v = [jnp.stack(a) for a in zip(*p_states)]
    s_sb_k, s_sb_v, s_cmp_k, s_cmp_v, s_slc_k, s_slc_v, s_win_k, s_win_v = [jnp.stack(a) for a in zip(*s_states)]
    y_prompt = rms_norm(y_p, g_final)
    y_sample = rms_norm(y_s, g_final)
    return (y_prompt, y_sample, p_sb_k, p_sb_v, p_cmp_k, p_cmp_v, p_slc_k, p_slc_v, p_win_k, p_win_v,
            s_sb_k, s_sb_v, s_cmp_k, s_cmp_v, s_slc_k, s_slc_v, s_win_k, s_win_v)
```

```python
import functools

import jax
import jax.numpy as jnp
import numpy as np
from jax import lax
from jax.experimental import pallas as pl
from jax.experimental.pallas import tpu as pltpu

F32 = jnp.float32
BF16 = jnp.bfloat16

HEAD_DIM = 128
LANES = 128
NSA_KV_HEADS = 4
NSA_GROUP = 4
CMP_LEN = 32
CMP_STRIDE = 16
CMP_HID = 2 * HEAD_DIM
SLC_LEN = 64
N_SEL = 16
WINDOW = 512
ROPE_THETA = 500000.0
ROPE_DIM = HEAD_DIM // 4
ROPE_HALF = ROPE_DIM // 2
RMS_EPS = 1e-6
NEG = -1e30
PAGE_SIZE = 128
SCALE = HEAD_DIM ** -0.5
VMEM_LIMIT = 56 * 1024 * 1024


def _cparams(sem):
    return pltpu.CompilerParams(dimension_semantics=sem, vmem_limit_bytes=VMEM_LIMIT)


def _dot(a, b):
    return jnp.dot(a, b, preferred_element_type=F32)


def _dot_nt(a, b):
    return lax.dot_general(a, b, (((1,), (1,)), ((), ())), preferred_element_type=F32)


def _dot_split(x, m_bf16):
    hi = x.astype(BF16)
    lo = (x - hi.astype(F32)).astype(BF16)
    return _dot(hi, m_bf16) + _dot(lo, m_bf16)


def _dot_split_left(m_bf16, x):
    hi = x.astype(BF16)
    lo = (x - hi.astype(F32)).astype(BF16)
    return _dot(m_bf16, hi) + _dot(m_bf16, lo)


def _idiv(x, d):
    return x >> (d.bit_length() - 1) if d & (d - 1) == 0 else x // d


def _imod(x, d):
    return x & (d - 1) if d & (d - 1) == 0 else x % d


def _softplus(z):
    return jnp.maximum(z, 0.0) + jnp.log1p(jnp.exp(-jnp.abs(z)))


def _rmsnorm_kernel(x_ref, g_ref, o_ref):
    x = x_ref[...]
    y = x * lax.rsqrt(jnp.mean(x * x, axis=-1, keepdims=True) + RMS_EPS)
    o_ref[...] = (y * g_ref[...]).astype(o_ref.dtype)


def rmsnorm(x, g, out_dtype, tm=256):
    m, d = x.shape
    return pl.pallas_call(
        _rmsnorm_kernel,
        grid=(m // tm,),
        in_specs=[pl.BlockSpec((tm, d), lambda i: (i, 0)),
                  pl.BlockSpec((1, d), lambda i: (0, 0))],
        out_specs=pl.BlockSpec((tm, d), lambda i: (i, 0)),
        out_shape=jax.ShapeDtypeStruct((m, d), out_dtype),
        compiler_params=_cparams(("parallel",)),
    )(x, g.reshape(1, d))


def _rope_tile(acc, c, s1, s2):
    return (acc * c + pltpu.roll(acc, LANES - ROPE_HALF, axis=1) * s1
            + pltpu.roll(acc, ROPE_HALF, axis=1) * s2)


def _mm_kernel(*refs, epi):
    if epi == "rope":
        x_ref, w_ref, c_ref, s1_ref, s2_ref, o_ref = refs
    elif epi == "residual":
        x_ref, w_ref, r_ref, o_ref = refs
    else:
        x_ref, w_ref, o_ref = refs
    acc = _dot(x_ref[...], w_ref[...])
    if epi == "rope":
        c, s1, s2 = c_ref[...], s1_ref[...], s2_ref[...]
        for hh in range(acc.shape[1] // LANES):
            sl = slice(hh * LANES, (hh + 1) * LANES)
            o_ref[:, sl] = _rope_tile(acc[:, sl], c, s1, s2).astype(o_ref.dtype)
    elif epi == "sigmoid":
        o_ref[...] = jax.nn.sigmoid(acc).astype(o_ref.dtype)
    elif epi == "residual":
        o_ref[...] = (r_ref[...] + acc).astype(o_ref.dtype)
    else:
        o_ref[...] = acc.astype(o_ref.dtype)


def mm(x, w, out_dtype, epi="plain", extras=(), tm=512, tn=512, table_rows=None):
    m, k = x.shape
    n = w.shape[1]
    tm, tn = min(tm, m), min(tn, n)
    in_specs = [pl.BlockSpec((tm, k), lambda i, j: (i, 0)),
                pl.BlockSpec((k, tn), lambda i, j: (0, j))]
    if epi == "rope":
        nblk = table_rows // tm
        in_specs += [pl.BlockSpec((tm, LANES), lambda i, j: (i % nblk, 0))] * 3
    elif epi == "residual":
        in_specs += [pl.BlockSpec((tm, tn), lambda i, j: (i, j))]
    return pl.pallas_call(
        functools.partial(_mm_kernel, epi=epi),
        grid=(m // tm, n // tn),
        in_specs=in_specs,
        out_specs=pl.BlockSpec((tm, tn), lambda i, j: (i, j)),
        out_shape=jax.ShapeDtypeStruct((m, n), out_dtype),
        compiler_params=_cparams(("parallel", "arbitrary")),
    )(x, w, *extras)


def _merge_kernel(a_ref, b_ref, wa_ref, wb_ref, ga_ref, gb_ref, o_ref):
    ya = _dot(a_ref[...], wa_ref[...])
    yb = _dot(b_ref[...], wb_ref[...])
    o_ref[...] = (jax.nn.sigmoid(ga_ref[...]) * ya + jax.nn.sigmoid(gb_ref[...]) * yb).astype(o_ref.dtype)


def merge(o_a, o_b, wa, wb, g_a, g_b, tm=512, tn=512):
    m, k = o_a.shape
    n = wa.shape[1]
    tm = min(tm, m)
    xs = pl.BlockSpec((tm, k), lambda i, j: (i, 0))
    ws = pl.BlockSpec((k, tn), lambda i, j: (0, j))
    gs = pl.BlockSpec((tm, tn), lambda i, j: (i, j))
    return pl.pallas_call(
        _merge_kernel,
        grid=(m // tm, n // tn),
        in_specs=[xs, xs, ws, ws, gs, gs],
        out_specs=gs,
        out_shape=jax.ShapeDtypeStruct((m, n), BF16),
        compiler_params=_cparams(("parallel", "arbitrary")),
    )(o_a, o_b, wa, wb, g_a, g_b)


def _ffn_up_kernel(x_ref, wg_ref, wu_ref, o_ref):
    x = x_ref[...]
    a = _dot(x, wg_ref[...])
    b = _dot(x, wu_ref[...])
    o_ref[...] = (a * jax.nn.sigmoid(a) * b).astype(o_ref.dtype)


def ffn_up(x, wg, wu, tm=512, tn=256):
    m, k = x.shape
    n = wg.shape[1]
    tm = min(tm, m)
    ws = pl.BlockSpec((k, tn), lambda i, j: (0, j))
    return pl.pallas_call(
        _ffn_up_kernel,
        grid=(m // tm, n // tn),
        in_specs=[pl.BlockSpec((tm, k), lambda i, j: (i, 0)), ws, ws],
        out_specs=pl.BlockSpec((tm, tn), lambda i, j: (i, j)),
        out_shape=jax.ShapeDtypeStruct((m, n), BF16),
        compiler_params=_cparams(("parallel", "arbitrary")),
    )(x, wg, wu)


def rope_tables(pos):
    inv = jnp.float32(ROPE_THETA) ** (-(jnp.arange(ROPE_HALF, dtype=F32) * 2.0 / ROPE_DIM))
    ang = pos.astype(F32)[:, None] * inv[None, :]
    cos, sin = jnp.cos(ang), jnp.sin(ang)
    n = pos.shape[0]
    rest = HEAD_DIM - ROPE_DIM
    c = jnp.concatenate([cos, cos, jnp.ones((n, rest), F32)], axis=1)
    s1 = jnp.concatenate([-sin, jnp.zeros((n, HEAD_DIM - ROPE_HALF), F32)], axis=1)
    s2 = jnp.concatenate([jnp.zeros((n, ROPE_HALF), F32), sin, jnp.zeros((n, rest), F32)], axis=1)
    return c, s1, s2


SB_BLK = 256


def _strict_lower(n):
    r = lax.broadcasted_iota(jnp.int32, (n, n), 0)
    c = lax.broadcasted_iota(jnp.int32, (n, n), 1)
    return jnp.where(r > c, 1.0, 0.0).astype(BF16)


def _sb_tile(q, k, v, tri, carry, mask):
    z = _dot_nt(q, k) * SCALE
    sp = _softplus(z)
    lg = -sp if mask is None else jnp.where(mask, -sp, 0.0)
    later = _dot_split(lg, tri) + carry
    a = jnp.exp(z - sp + later)
    if mask is not None:
        a = jnp.where(mask, a, 0.0)
    return _dot(a.astype(BF16), v), carry + jnp.sum(lg, axis=1, keepdims=True)


def _sb_prompt_kernel(q_ref, k_ref, v_ref, o_ref, kb_ref, vb_ref):
    t = q_ref.shape[1]
    kb_ref[...] = k_ref[0].astype(BF16)
    vb_ref[...] = v_ref[0].astype(BF16)
    tri = _strict_lower(SB_BLK)
    r = lax.broadcasted_iota(jnp.int32, (SB_BLK, SB_BLK), 0)
    c = lax.broadcasted_iota(jnp.int32, (SB_BLK, SB_BLK), 1)
    diag_mask = c < r

    def q_block(qi, _):
        q0 = pl.multiple_of(qi * SB_BLK, SB_BLK)
        q = q_ref[0, pl.ds(q0, SB_BLK), :]
        acc, carry = _sb_tile(q, kb_ref[pl.ds(q0, SB_BLK), :], vb_ref[pl.ds(q0, SB_BLK), :], tri,
                              jnp.zeros((SB_BLK, 1), F32), diag_mask)

        def k_block(step, st):
            acc, carry = st
            k0 = pl.multiple_of((qi - 1 - step) * SB_BLK, SB_BLK)
            o, carry = _sb_tile(q, kb_ref[pl.ds(k0, SB_BLK), :], vb_ref[pl.ds(k0, SB_BLK), :], tri, carry, None)
            return acc + o, carry

        acc, _ = lax.fori_loop(0, qi, k_block, (acc, carry))
        o_ref[0, pl.ds(q0, SB_BLK), :] = acc.astype(o_ref.dtype)
        return 0

    lax.fori_loop(0, t // SB_BLK, q_block, 0)


def sb_prompt(q, k, v, n_heads):
    b, t, _ = q.shape
    spec = pl.BlockSpec((1, t, HEAD_DIM), lambda bi, h: (bi, 0, h))
    return pl.pallas_call(
        _sb_prompt_kernel,
        grid=(b, n_heads),
        in_specs=[spec, spec, spec],
        out_specs=spec,
        out_shape=jax.ShapeDtypeStruct(q.shape, BF16),
        scratch_shapes=[pltpu.VMEM((t, HEAD_DIM), BF16), pltpu.VMEM((t, HEAD_DIM), BF16)],
        compiler_params=_cparams(("parallel", "parallel")),
    )(q, k, v)


def _compress_rows(x_ref, w1_ref, bias_ref, w2_ref, acc_ref):
    n_chunk = x_ref.shape[0] // (CMP_STRIDE * NSA_KV_HEADS)
    for s in range(CMP_STRIDE):
        lhs = jnp.concatenate(
            [x_ref[pl.ds(s * NSA_KV_HEADS + h, n_chunk, stride=CMP_STRIDE * NSA_KV_HEADS), :].astype(BF16)
             for h in range(NSA_KV_HEADS)], axis=0)
        part = _dot(lhs, w1_ref[s])
        if s == 0:
            acc_ref[...] = part
        else:
            acc_ref[...] += part
    rows = NSA_KV_HEADS * n_chunk
    u0 = acc_ref[:, :CMP_HID]
    u1 = acc_ref[:, CMP_HID:]
    hid = u0 + pltpu.roll(u1, rows - 1, axis=0) + bias_ref[0:1, :]
    hid = hid * jax.nn.sigmoid(hid)
    out = _dot(hid.astype(BF16), w2_ref[...])
    chunk = _imod(lax.broadcasted_iota(jnp.int32, (rows, HEAD_DIM), 0), n_chunk)
    return jnp.where(chunk < n_chunk - 1, out, 0.0)


def _compress_kernel(x_ref, w1_ref, bias_ref, w2_ref, o_ref, acc_ref):
    o_ref[0] = _compress_rows(x_ref.at[0], w1_ref, bias_ref, w2_ref, acc_ref).astype(o_ref.dtype)


def compress_prompt(x, w1r, bias, w2):
    b, t, _ = x.shape
    rows = NSA_KV_HEADS * (t // CMP_STRIDE)
    x = x.reshape(b, t * NSA_KV_HEADS, HEAD_DIM)
    return pl.pallas_call(
        _compress_kernel,
        grid=(b,),
        in_specs=[pl.BlockSpec((1, t * NSA_KV_HEADS, HEAD_DIM), lambda i: (i, 0, 0)),
                  pl.BlockSpec(w1r.shape, lambda i: (0, 0, 0)),
                  pl.BlockSpec(bias.shape, lambda i: (0, 0)),
                  pl.BlockSpec(w2.shape, lambda i: (0, 0))],
        out_specs=pl.BlockSpec((1, rows, HEAD_DIM), lambda i: (i, 0, 0)),
        out_shape=jax.ShapeDtypeStruct((b, rows, HEAD_DIM), BF16),
        scratch_shapes=[pltpu.VMEM((rows, 2 * CMP_HID), F32)],
        compiler_params=_cparams(("parallel",)),
    )(x, w1r, bias, w2)


def prep_compress_weights(w1, pe, w2):
    r = CMP_LEN // CMP_STRIDE
    w1r = w1.reshape(r, CMP_STRIDE, HEAD_DIM, CMP_HID).transpose(1, 2, 0, 3)
    w1r = w1r.reshape(CMP_STRIDE, HEAD_DIM, r * CMP_HID).astype(BF16)
    pe_rows = jnp.zeros((16, CMP_LEN * HEAD_DIM), F32).at[0].set(pe.reshape(-1)).astype(BF16)
    bias = mm(pe_rows, w1.reshape(CMP_LEN * HEAD_DIM, CMP_HID).astype(BF16), F32, tn=CMP_HID)
    return w1r, bias, w2.astype(BF16)


def cmp_to_slc_matrix(n_cmp_pad, n_cmp, n_slc):
    i = np.arange(n_cmp_pad)[:, None]
    j = np.arange(LANES)[None, :]
    lo = np.maximum(i * CMP_STRIDE, j * SLC_LEN)
    hi = np.minimum(i * CMP_STRIDE + CMP_LEN, j * SLC_LEN + SLC_LEN)
    m = np.clip(hi - lo, 0, None) / CMP_LEN
    m = np.where((i < n_cmp) & (j < n_slc), m, 0.0)
    return jnp.asarray(m.astype(np.float32)).astype(BF16)


def _masked_softmax(s, mask):
    s = jnp.where(mask, s, NEG)
    m = jnp.max(s, axis=1, keepdims=True)
    e = jnp.where(mask, jnp.exp(s - m), 0.0)
    den = jnp.sum(e, axis=1, keepdims=True)
    return e / jnp.where(den > 0.0, den, 1.0)


def _select_blocks(imp, qpos, n_slc):
    blk = lax.broadcasted_iota(jnp.int32, imp.shape, 1)
    cur = _idiv(qpos, SLC_LEN)
    valid = (blk * SLC_LEN <= qpos) & (blk < n_slc)
    forced = (blk == 0) | (blk == cur) | (blk == cur - 1)
    score = jnp.where(valid, jnp.where(forced, jnp.inf, imp), -jnp.inf)
    rank = jnp.zeros(imp.shape, F32)
    for i in range(n_slc):
        col = score[:, i:i + 1]
        ge = jnp.where(col >= score, 1.0, 0.0)
        gt = jnp.where(col > score, 1.0, 0.0)
        rank = rank + jnp.where(blk > i, ge, gt)
    return jnp.where(valid & (rank < float(min(N_SEL, n_slc))), 1.0, 0.0)


NSA_QB = 128
NSA_KB = 256


def _nsa_prompt_kernel(q_ref, ck_ref, cv_ref, ks_ref, vs_ref, kw_ref, vw_ref, g_ref, m_ref, o_ref,
                       ksb, vsb, kwb, vwb):
    t = ks_ref.shape[1]
    qb = pl.program_id(2)
    n_slc = t // SLC_LEN

    @pl.when(qb == 0)
    def _():
        ksb[...] = ks_ref[0].astype(BF16)
        vsb[...] = vs_ref[0].astype(BF16)
        kwb[...] = kw_ref[0].astype(BF16)
        vwb[...] = vw_ref[0].astype(BF16)

    rows = NSA_GROUP * NSA_QB
    q0 = qb * NSA_QB
    q = jnp.concatenate([q_ref[0, :, g * HEAD_DIM:(g + 1) * HEAD_DIM] for g in range(NSA_GROUP)], axis=0)
    qpos1 = q0 + lax.broadcasted_iota(jnp.int32, (NSA_QB, 1), 0)
    qpos = jnp.concatenate([qpos1] * NSA_GROUP, axis=0)

    n_cmp_pad = ck_ref.shape[2]
    s_c = _dot_nt(q, ck_ref[0, 0]) * SCALE
    cmp_end = lax.broadcasted_iota(jnp.int32, (rows, n_cmp_pad), 1) * CMP_STRIDE + (CMP_LEN - 1)
    p_c = _masked_softmax(s_c, cmp_end <= qpos)
    o_c = _dot(p_c.astype(BF16), cv_ref[0, 0])

    p_sum = p_c[0:NSA_QB]
    for g in range(1, NSA_GROUP):
        p_sum = p_sum + p_c[g * NSA_QB:(g + 1) * NSA_QB]
    sel = _select_blocks(_dot_split(p_sum, m_ref[...]), qpos1, n_slc).astype(BF16)

    def slc_chunk(kc, st):
        m_i, l_i, acc = st
        k0 = pl.multiple_of(kc * NSA_KB, NSA_KB)
        blk_of_key = _idiv(k0 + lax.broadcasted_iota(jnp.int32, (LANES, NSA_KB), 1), SLC_LEN)
        expand = jnp.where(lax.broadcasted_iota(jnp.int32, (LANES, NSA_KB), 0) == blk_of_key, 1.0, 0.0)
        chosen = _dot(sel, expand.astype(BF16))
        kpos = k0 + lax.broadcasted_iota(jnp.int32, (NSA_QB, NSA_KB), 1)
        allowed = jnp.where(kpos <= qpos1, chosen, 0.0)
        mask = jnp.concatenate([allowed] * NSA_GROUP, axis=0) > 0.5
        s = jnp.where(mask, _dot_nt(q, ksb[pl.ds(k0, NSA_KB), :]) * SCALE, NEG)
        m_new = jnp.maximum(m_i, jnp.max(s, axis=1, keepdims=True))
        alpha = jnp.exp(m_i - m_new)
        e = jnp.where(mask, jnp.exp(s - m_new), 0.0)
        l_new = alpha * l_i + jnp.sum(e, axis=1, keepdims=True)
        acc = alpha * acc + _dot(e.astype(BF16), vsb[pl.ds(k0, NSA_KB), :])
        return m_new, l_new, acc

    n_chunks = (q0 + NSA_QB + NSA_KB - 1) // NSA_KB
    m_i, l_i, acc = lax.fori_loop(
        0, n_chunks, slc_chunk,
        (jnp.full((rows, 1), NEG, F32), jnp.zeros((rows, 1), F32), jnp.zeros((rows, HEAD_DIM), F32)))
    o_s = acc / jnp.where(l_i > 0.0, l_i, 1.0)

    band = WINDOW + NSA_QB
    w0 = pl.multiple_of(jnp.maximum(q0 - WINDOW, 0), NSA_QB)
    kpos = w0 + lax.broadcasted_iota(jnp.int32, (rows, band), 1)
    dist = qpos - kpos
    s_w = _dot_nt(q, kwb[pl.ds(w0, band), :]) * SCALE
    p_w = _masked_softmax(s_w, (dist >= 0) & (dist < WINDOW))
    o_w = _dot(p_w.astype(BF16), vwb[pl.ds(w0, band), :])

    gates = g_ref[0, 0]
    for g in range(NSA_GROUP):
        rs = slice(g * NSA_QB, (g + 1) * NSA_QB)
        o = (gates[:, 3 * g:3 * g + 1] * o_c[rs] + gates[:, 3 * g + 1:3 * g + 2] * o_s[rs]
             + gates[:, 3 * g + 2:3 * g + 3] * o_w[rs])
        o_ref[0, :, g * HEAD_DIM:(g + 1) * HEAD_DIM] = o.astype(o_ref.dtype)


def nsa_prompt(q, ck, cv, ks, vs, kw, vw, gates, m_mat):
    b, t, _ = q.shape
    gw = NSA_GROUP * HEAD_DIM
    qspec = pl.BlockSpec((1, NSA_QB, gw), lambda bi, h, qb: (bi, qb, h))
    cspec = pl.BlockSpec((1, 1, ck.shape[2], HEAD_DIM), lambda bi, h, qb: (bi, h, 0, 0))
    kspec = pl.BlockSpec((1, t, HEAD_DIM), lambda bi, h, qb: (bi, 0, h))
    return pl.pallas_call(
        _nsa_prompt_kernel,
        grid=(b, NSA_KV_HEADS, t // NSA_QB),
        in_specs=[qspec, cspec, cspec, kspec, kspec, kspec, kspec,
                  pl.BlockSpec((1, 1, NSA_QB, LANES), lambda bi, h, qb: (bi, h, qb, 0)),
                  pl.BlockSpec(m_mat.shape, lambda bi, h, qb: (0, 0))],
        out_specs=qspec,
        out_shape=jax.ShapeDtypeStruct(q.shape, BF16),
        scratch_shapes=[pltpu.VMEM((t, HEAD_DIM), BF16)] * 4,
        compiler_params=_cparams(("parallel", "parallel", "arbitrary")),
    )(q, ck, cv, ks, vs, kw, vw, gates, m_mat)


def _diag_heads(acc, n_heads, rows_per_head):
    return jnp.concatenate(
        [acc[h * rows_per_head:(h + 1) * rows_per_head, h * HEAD_DIM:(h + 1) * HEAD_DIM]
         for h in range(n_heads)], axis=0)


def _sb_sample_kernel(pt_ref, q_ref, kp_ref, vp_ref, kn_ref, vn_ref, o_ref, acc_ref, carry_ref, *, n_heads, dec_seq):
    p = pl.program_id(1)
    n_steps = pl.num_programs(1)
    rows = q_ref.shape[1]
    tri = _strict_lower(PAGE_SIZE)

    def step(k, v, mask):
        o, carry = _sb_tile(q_ref[0], k, v, tri, carry_ref[...], mask)
        acc_ref[...] += o
        carry_ref[...] = carry

    @pl.when(p == 0)
    def _():
        acc_ref[...] = jnp.zeros_like(acc_ref)
        carry_ref[...] = jnp.zeros_like(carry_ref)
        t_row = _imod(lax.broadcasted_iota(jnp.int32, (rows, PAGE_SIZE), 0), dec_seq)
        j = lax.broadcasted_iota(jnp.int32, (rows, PAGE_SIZE), 1)
        step(kn_ref[0], vn_ref[0], j < t_row)

    @pl.when(p > 0)
    def _():
        step(kp_ref[0].astype(BF16), vp_ref[0].astype(BF16), None)

    @pl.when(p == n_steps - 1)
    def _():
        o_ref[0] = _diag_heads(acc_ref[...], n_heads, dec_seq).astype(o_ref.dtype)


def sb_sample(page_table, q_bd, k_pool, v_pool, k_new, v_new, n_heads, dec_seq):
    b, rows, w = q_bd.shape
    n_pages = page_table.shape[1]
    pool_spec = pl.BlockSpec((1, PAGE_SIZE, w),
                             lambda bi, p, pt: (pt[bi, n_pages - jnp.maximum(p, 1)], 0, 0))
    new_spec = pl.BlockSpec((1, PAGE_SIZE, w), lambda bi, p, pt: (bi, 0, 0))
    grid_spec = pltpu.PrefetchScalarGridSpec(
        num_scalar_prefetch=1,
        grid=(b, n_pages + 1),
        in_specs=[pl.BlockSpec((1, rows, w), lambda bi, p, pt: (bi, 0, 0)),
                  pool_spec, pool_spec, new_spec, new_spec],
        out_specs=pl.BlockSpec((1, rows, HEAD_DIM), lambda bi, p, pt: (bi, 0, 0)),
        scratch_shapes=[pltpu.VMEM((rows, w), F32), pltpu.VMEM((rows, 1), F32)],
    )
    return pl.pallas_call(
        functools.partial(_sb_sample_kernel, n_heads=n_heads, dec_seq=dec_seq),
        grid_spec=grid_spec,
        out_shape=jax.ShapeDtypeStruct((b, rows, HEAD_DIM), BF16),
        compiler_params=_cparams(("parallel", "arbitrary")),
    )(page_table, q_bd, k_pool, v_pool, k_new, v_new)


def _cmp_sample_kernel(pt_ref, q_ref, kp_ref, vp_ref, w1k_ref, bk_ref, w2k_ref, w1v_ref, bv_ref, w2v_ref,
                       m_ref, oc_ref, sel_ref, kbuf, vbuf, acc_ref, *, dec_seq, past_len):
    p = pl.program_id(1)
    n_pages = pl.num_programs(1)
    page_rows = kp_ref.shape[1]
    r0 = pl.multiple_of(p * page_rows, page_rows)
    kbuf[pl.ds(r0, page_rows), :] = kp_ref[0]
    vbuf[pl.ds(r0, page_rows), :] = vp_ref[0]

    @pl.when(p == n_pages - 1)
    def _():
        n_chunk = past_len // CMP_STRIDE
        n_slc = -(-(past_len + dec_seq) // SLC_LEN)
        ck = _compress_rows(kbuf, w1k_ref, bk_ref, w2k_ref, acc_ref).astype(BF16)
        cv = _compress_rows(vbuf, w1v_ref, bv_ref, w2v_ref, acc_ref).astype(BF16)
        rows = NSA_GROUP * dec_seq
        t_row = _imod(lax.broadcasted_iota(jnp.int32, (rows, 1), 0), dec_seq)
        qpos = past_len + t_row
        cmp_end = lax.broadcasted_iota(jnp.int32, (rows, n_chunk), 1) * CMP_STRIDE + (CMP_LEN - 1)
        r = lax.broadcasted_iota(jnp.int32, (rows, rows), 0)
        c = lax.broadcasted_iota(jnp.int32, (rows, rows), 1)
        same_t = jnp.where(_imod(r, dec_seq) == _imod(c, dec_seq), 1.0, 0.0).astype(BF16)
        for h in range(NSA_KV_HEADS):
            q = q_ref[0, h * rows:(h + 1) * rows, :]
            cs = slice(h * n_chunk, (h + 1) * n_chunk)
            p_c = _masked_softmax(_dot_nt(q, ck[cs]) * SCALE, cmp_end <= qpos)
            oc_ref[0, h * rows:(h + 1) * rows, :] = _dot(p_c.astype(BF16), cv[cs])
            p_sum = _dot_split_left(same_t, p_c)
            imp = _dot_split(p_sum, m_ref[...])
            sel_ref[0, h * rows:(h + 1) * rows, :] = _select_blocks(imp, qpos, n_slc)


def cmp_sample(page_table, q_rows, k_pool, v_pool, wk, wv, m_mat, dec_seq):
    b, rows, _ = q_rows.shape
    n_pages = page_table.shape[1]
    past_len = n_pages * PAGE_SIZE
    page_rows = PAGE_SIZE * NSA_KV_HEADS
    n_chunk = past_len // CMP_STRIDE
    pool_spec = pl.BlockSpec((1, page_rows, HEAD_DIM), lambda bi, p, pt: (pt[bi, p], 0, 0))
    const = lambda a: pl.BlockSpec(a.shape, lambda bi, p, pt: (0,) * a.ndim)
    row_spec = pl.BlockSpec((1, rows, HEAD_DIM), lambda bi, p, pt: (bi, 0, 0))
    grid_spec = pltpu.PrefetchScalarGridSpec(
        num_scalar_prefetch=1,
        grid=(b, n_pages),
        in_specs=[row_spec, pool_spec, pool_spec, *[const(a) for a in (*wk, *wv, m_mat)]],
        out_specs=[row_spec, row_spec],
        scratch_shapes=[pltpu.VMEM((n_pages * page_rows, HEAD_DIM), F32),
                        pltpu.VMEM((n_pages * page_rows, HEAD_DIM), F32),
                        pltpu.VMEM((NSA_KV_HEADS * n_chunk, 2 * CMP_HID), F32)],
    )
    return pl.pallas_call(
        functools.partial(_cmp_sample_kernel, dec_seq=dec_seq, past_len=past_len),
        grid_spec=grid_spec,
        out_shape=[jax.ShapeDtypeStruct((b, rows, HEAD_DIM), F32)] * 2,
        compiler_params=_cparams(("parallel", "arbitrary")),
    )(page_table, q_rows, k_pool, v_pool, *wk, *wv, m_mat)


def _slc_sample_kernel(pt_ref, q_ref, kp_ref, vp_ref, kn_ref, vn_ref, sel_ref, o_ref, m_ref, l_ref, acc_ref,
                       *, dec_seq):
    p = pl.program_id(1)
    n_pages = pl.num_programs(1) - 1
    rows = q_ref.shape[1]
    blocks_per_page = PAGE_SIZE // SLC_LEN

    @pl.when(p == 0)
    def _():
        m_ref[...] = jnp.full_like(m_ref, NEG)
        l_ref[...] = jnp.zeros_like(l_ref)
        acc_ref[...] = jnp.zeros_like(acc_ref)

    def step(k, v, causal):
        key = lax.broadcasted_iota(jnp.int32, (LANES, PAGE_SIZE), 1)
        blk_of_key = p * blocks_per_page + _idiv(key, SLC_LEN)
        expand = jnp.where(lax.broadcasted_iota(jnp.int32, (LANES, PAGE_SIZE), 0) == blk_of_key, 1.0, 0.0)
        mask = _dot(sel_ref[0].astype(BF16), expand.astype(BF16)) > 0.5
        if causal is not None:
            mask = mask & causal
        s = jnp.where(mask, _dot_nt(q_ref[0], k) * SCALE, NEG)
        m_new = jnp.maximum(m_ref[...], jnp.max(s, axis=1, keepdims=True))
        alpha = jnp.exp(m_ref[...] - m_new)
        e = jnp.where(mask, jnp.exp(s - m_new), 0.0)
        l_ref[...] = alpha * l_ref[...] + jnp.sum(e, axis=1, keepdims=True)
        acc_ref[...] = alpha * acc_ref[...] + _dot(e.astype(BF16), v)
        m_ref[...] = m_new

    @pl.when(p < n_pages)
    def _():
        step(kp_ref[0].astype(BF16), vp_ref[0].astype(BF16), None)

    @pl.when(p == n_pages)
    def _():
        t_row = _imod(lax.broadcasted_iota(jnp.int32, (rows, PAGE_SIZE), 0), dec_seq)
        j = lax.broadcasted_iota(jnp.int32, (rows, PAGE_SIZE), 1)
        step(kn_ref[0], vn_ref[0], j <= t_row)
        l = l_ref[...]
        o = acc_ref[...] / jnp.where(l > 0.0, l, 1.0)
        o_ref[0] = _diag_heads(o, NSA_KV_HEADS, rows // NSA_KV_HEADS)


def slc_sample(page_table, q_bd, k_pool, v_pool, k_new, v_new, sel, dec_seq):
    b, rows, w = q_bd.shape
    n_pages = page_table.shape[1]
    pool_spec = pl.BlockSpec((1, PAGE_SIZE, w), lambda bi, p, pt: (pt[bi, jnp.minimum(p, n_pages - 1)], 0, 0))
    new_spec = pl.BlockSpec((1, PAGE_SIZE, w), lambda bi, p, pt: (bi, 0, 0))
    row_spec = pl.BlockSpec((1, rows, HEAD_DIM), lambda bi, p, pt: (bi, 0, 0))
    grid_spec = pltpu.PrefetchScalarGridSpec(
        num_scalar_prefetch=1,
        grid=(b, n_pages + 1),
        in_specs=[pl.BlockSpec((1, rows, w), lambda bi, p, pt: (bi, 0, 0)),
                  pool_spec, pool_spec, new_spec, new_spec, row_spec],
        out_specs=row_spec,
        scratch_shapes=[pltpu.VMEM((rows, 1), F32), pltpu.VMEM((rows, 1), F32), pltpu.VMEM((rows, w), F32)],
    )
    return pl.pallas_call(
        functools.partial(_slc_sample_kernel, dec_seq=dec_seq),
        grid_spec=grid_spec,
        out_shape=jax.ShapeDtypeStruct((b, rows, HEAD_DIM), F32),
        compiler_params=_cparams(("parallel", "arbitrary")),
    )(page_table, q_bd, k_pool, v_pool, k_new, v_new, sel)


def _win_sample_kernel(q_ref, kw_ref, vw_ref, kn_ref, vn_ref, oc_ref, os_ref, g_ref, o_ref, *, dec_seq):
    rows = q_ref.shape[1]
    n_buf = kw_ref.shape[1]
    q = q_ref[0]
    k = jnp.concatenate([kw_ref[0].astype(BF16), kn_ref[0]], axis=0)
    v = jnp.concatenate([vw_ref[0].astype(BF16), vn_ref[0]], axis=0)
    n_keys = k.shape[0]
    t_row = _imod(lax.broadcasted_iota(jnp.int32, (rows, n_keys), 0), dec_seq)
    j = lax.broadcasted_iota(jnp.int32, (rows, n_keys), 1)
    off = j - n_buf
    mask = (off <= t_row) & (off > t_row - WINDOW) & (off < dec_seq)
    p_w = _masked_softmax(_dot_nt(q, k) * SCALE, mask)
    o_w = _diag_heads(_dot(p_w.astype(BF16), v), NSA_KV_HEADS, rows // NSA_KV_HEADS)
    g = g_ref[0]
    o_ref[0] = (g[:, 0:1] * oc_ref[0] + g[:, 1:2] * os_ref[0] + g[:, 2:3] * o_w).astype(o_ref.dtype)


def win_sample(q_bd, kw, vw, k_new, v_new, o_c, o_s, gates, dec_seq):
    b, rows, w = q_bd.shape
    n_buf = kw.shape[1]
    row_spec = pl.BlockSpec((1, rows, HEAD_DIM), lambda bi: (bi, 0, 0))
    buf_spec = pl.BlockSpec((1, n_buf, w), lambda bi: (bi, 0, 0))
    new_spec = pl.BlockSpec((1, PAGE_SIZE, w), lambda bi: (bi, 0, 0))
    return pl.pallas_call(
        functools.partial(_win_sample_kernel, dec_seq=dec_seq),
        grid=(b,),
        in_specs=[pl.BlockSpec((1, rows, w), lambda bi: (bi, 0, 0)), buf_spec, buf_spec, new_spec, new_spec,
                  row_spec, row_spec, row_spec],
        out_specs=row_spec,
        out_shape=jax.ShapeDtypeStruct((b, rows, HEAD_DIM), BF16),
        compiler_params=_cparams(("parallel",)),
    )(q_bd, kw, vw, k_new, v_new, o_c, o_s, gates)


def prep_weights(w_in, w_branch_a, w_branch_b, w_out, w_ffn_gate, w_ffn_up, w_ffn_down, sb_w, nsa_w, kv_w, d_model):
    n_gate = 3 * (nsa_w // HEAD_DIM)
    widths = (sb_w, sb_w, sb_w, nsa_w, kv_w, kv_w, kv_w, kv_w, kv_w, kv_w, n_gate, d_model, d_model)
    offs = np.concatenate([[0], np.cumsum(widths)])
    parts = [w_in[:, int(offs[i]):int(offs[i + 1])].astype(BF16) for i in range(len(widths))]
    parts[10] = jnp.pad(parts[10], ((0, 0), (0, LANES - n_gate)))
    return dict(w_in=parts, wa=w_branch_a.astype(BF16), wb=w_branch_b.astype(BF16), w_out=w_out.astype(BF16),
                wg=w_ffn_gate.astype(BF16), wu=w_ffn_up.astype(BF16), wd=w_ffn_down.astype(BF16))


def projections(x2d, g_mix, w_parts, tables, table_rows):
    xn = rmsnorm(x2d, g_mix, BF16)
    rope = lambda w, dt: mm(xn, w, dt, epi="rope", extras=tables, table_rows=table_rows)
    plain = lambda w, dt: mm(xn, w, dt)
    q_sb, k_sb, v_sb = plain(w_parts[0], BF16), plain(w_parts[1], F32), plain(w_parts[2], F32)
    q_n = rope(w_parts[3], BF16)
    k_c, v_c = rope(w_parts[4], F32), plain(w_parts[5], F32)
    k_s, v_s = rope(w_parts[6], F32), plain(w_parts[7], F32)
    k_w, v_w = rope(w_parts[8], F32), plain(w_parts[9], F32)
    g_n = mm(xn, w_parts[10], F32, epi="sigmoid")
    g_a, g_b = plain(w_parts[11], F32), plain(w_parts[12], F32)
    return q_sb, k_sb, v_sb, q_n, k_c, v_c, k_s, v_s, k_w, v_w, g_n, g_a, g_b


def merge_and_ffn(x2d, o_sb, o_n, g_a, g_b, wts, g_ffn, g_final):
    mixed = merge(o_sb, o_n, wts["wa"], wts["wb"], g_a, g_b)
    h = mm(mixed, wts["w_out"], F32, epi="residual", extras=(x2d,))
    hn = rmsnorm(h, g_ffn, BF16)
    act = ffn_up(hn, wts["wg"], wts["wu"])
    y = mm(act, wts["wd"], F32, epi="residual", extras=(h,), tn=256)
    return rmsnorm(y, g_final, F32)


def prompt_layer(x, wts, cw_k, cw_v, g_mix, g_ffn, g_final):
    b, t, d = x.shape
    x2d = x.reshape(b * t, d)
    tables = rope_tables(jnp.arange(t))
    (q_sb, k_sb, v_sb, q_n, k_c, v_c, k_s, v_s, k_w, v_w, g_n, g_a, g_b) = projections(
        x2d, g_mix, wts["w_in"], tables, t)
    r3 = lambda a: a.reshape(b, t, -1)
    n_sb_heads = q_sb.shape[1] // HEAD_DIM
    o_sb = sb_prompt(r3(q_sb), r3(k_sb), r3(v_sb), n_sb_heads)

    n_chunk = t // CMP_STRIDE
    ck = compress_prompt(r3(k_c), *cw_k).reshape(b, NSA_KV_HEADS, n_chunk, HEAD_DIM)
    cv = compress_prompt(r3(v_c), *cw_v).reshape(b, NSA_KV_HEADS, n_chunk, HEAD_DIM)
    n_gate = 3 * NSA_GROUP
    gates = g_n[:, :NSA_KV_HEADS * n_gate].reshape(b, t, NSA_KV_HEADS, n_gate).transpose(0, 2, 1, 3)
    gates = jnp.pad(gates, ((0, 0), (0, 0), (0, 0), (0, LANES - n_gate)))
    m_mat = cmp_to_slc_matrix(n_chunk, n_chunk - CMP_LEN // CMP_STRIDE + 1, t // SLC_LEN)
    o_n = nsa_prompt(r3(q_n), ck, cv, r3(k_s), r3(v_s), r3(k_w), r3(v_w), gates, m_mat)

    y = merge_and_ffn(x2d, o_sb.reshape(b * t, -1), o_n.reshape(b * t, -1), g_a, g_b, wts, g_ffn, g_final)
    heads = lambda a: a.reshape(1, b, t, -1, HEAD_DIM)
    keep = min(WINDOW, t)
    states = (heads(k_sb), heads(v_sb), heads(k_c), heads(v_c), heads(k_s), heads(v_s),
              heads(k_w)[:, :, t - keep:], heads(v_w)[:, :, t - keep:])
    return y.reshape(b, t, d), states


def _block_diag_rows(q, n_heads):
    b, r = q.shape[:2]
    qt = q.transpose(0, 2, 1, 3)
    eye = jnp.eye(n_heads, dtype=q.dtype)
    return (qt[:, :, :, None, :] * eye[None, :, None, :, None]).reshape(b, n_heads * r, n_heads * HEAD_DIM)


def _pad_new(a, b, t):
    return jnp.pad(a.reshape(b, t, -1).astype(BF16), ((0, 0), (0, PAGE_SIZE - t), (0, 0)))


def sample_layer(x, page_table, caches, wts, cw_k, cw_v, g_mix, g_ffn, g_final):
    c_sb_k, c_sb_v, c_cmp_k, c_cmp_v, c_slc_k, c_slc_v, w_buf_k, w_buf_v = caches
    b, t, d = x.shape
    n_pages = page_table.shape[1]
    past_len = n_pages * PAGE_SIZE
    x2d = x.reshape(b * t, d)
    tables = rope_tables(past_len + jnp.arange(b * t) % t)
    (q_sb, k_sb, v_sb, q_n, k_c, v_c, k_s, v_s, k_w, v_w, g_n, g_a, g_b) = projections(
        x2d, g_mix, wts["w_in"], tables, b * t)
    pool2d = lambda a: a.reshape(a.shape[0], PAGE_SIZE, -1)

    n_sb_heads = q_sb.shape[1] // HEAD_DIM
    q_bd = _block_diag_rows(q_sb.reshape(b, t, n_sb_heads, HEAD_DIM), n_sb_heads)
    o_sb = sb_sample(page_table, q_bd, pool2d(c_sb_k), pool2d(c_sb_v), _pad_new(k_sb, b, t), _pad_new(v_sb, b, t),
                     n_sb_heads, t)
    o_sb = o_sb.reshape(b, n_sb_heads, t, HEAD_DIM).transpose(0, 2, 1, 3).reshape(b * t, -1)

    qn5 = q_n.reshape(b, t, NSA_KV_HEADS, NSA_GROUP, HEAD_DIM).transpose(0, 2, 3, 1, 4)
    rows = NSA_KV_HEADS * NSA_GROUP * t
    q_rows = qn5.reshape(b, rows, HEAD_DIM)
    qn_bd = _block_diag_rows(qn5.reshape(b, NSA_KV_HEADS, NSA_GROUP * t, HEAD_DIM).transpose(0, 2, 1, 3), NSA_KV_HEADS)
    n_chunk = past_len // CMP_STRIDE
    n_slc = -(-(past_len + t) // SLC_LEN)
    m_mat = cmp_to_slc_matrix(n_chunk, n_chunk - CMP_LEN // CMP_STRIDE + 1, n_slc)
    pool_rows = lambda a: a.reshape(a.shape[0], -1, HEAD_DIM)
    o_c, sel = cmp_sample(page_table, q_rows, pool_rows(c_cmp_k), pool_rows(c_cmp_v), cw_k, cw_v, m_mat, t)
    o_s = slc_sample(page_table, qn_bd, pool2d(c_slc_k), pool2d(c_slc_v), _pad_new(k_s, b, t), _pad_new(v_s, b, t),
                     sel, t)
    gates = g_n[:, :rows // t * 3].reshape(b, t, NSA_KV_HEADS, NSA_GROUP, 3).transpose(0, 2, 3, 1, 4)
    gates = jnp.pad(gates.reshape(b, rows, 3), ((0, 0), (0, 0), (0, LANES - 3)))
    n_buf = w_buf_k.shape[1]
    o_n = win_sample(qn_bd, w_buf_k.reshape(b, n_buf, -1), w_buf_v.reshape(b, n_buf, -1),
                     _pad_new(k_w, b, t), _pad_new(v_w, b, t), o_c, o_s, gates, t)
    o_n = o_n.reshape(b, NSA_KV_HEADS, NSA_GROUP, t, HEAD_DIM).transpose(0, 3, 1, 2, 4).reshape(b * t, -1)

    y = merge_and_ffn(x2d, o_sb, o_n, g_a, g_b, wts, g_ffn, g_final)
    heads = lambda a: a.reshape(1, b, t, -1, HEAD_DIM)
    wk = jnp.concatenate([w_buf_k[None], heads(k_w)], axis=2)[:, :, t:]
    wv = jnp.concatenate([w_buf_v[None], heads(v_w)], axis=2)[:, :, t:]
    states = (heads(k_sb), heads(v_sb), heads(k_c), heads(v_c), heads(k_s), heads(v_s), wk, wv)
    return y.reshape(b, t, d), states


def kernel(x_prompt, x_sample, cache_sb_k, cache_sb_v, cache_cmp_k, cache_cmp_v, cache_slc_k, cache_slc_v,
           state_win_k, state_win_v, page_table, g_mix, w_in, w_cmp_k1, w_cmp_k2, pe_cmp_k, w_cmp_v1, w_cmp_v2,
           pe_cmp_v, w_branch_a, w_branch_b, w_out, g_ffn, w_ffn_gate, w_ffn_up, w_ffn_down, g_final):
    assert w_in.shape[0] == 1, "single-layer trunk"
    d_model = x_prompt.shape[-1]
    sb_w = cache_sb_k.shape[3] * HEAD_DIM
    kv_w = NSA_KV_HEADS * HEAD_DIM
    nsa_w = NSA_KV_HEADS * NSA_GROUP * HEAD_DIM
    wts = prep_weights(w_in[0], w_branch_a[0], w_branch_b[0], w_out[0], w_ffn_gate[0], w_ffn_up[0],
                       w_ffn_down[0], sb_w, nsa_w, kv_w, d_model)
    cw_k = prep_compress_weights(w_cmp_k1[0], pe_cmp_k[0], w_cmp_k2[0])
    cw_v = prep_compress_weights(w_cmp_v1[0], pe_cmp_v[0], w_cmp_v2[0])
    y_p, st_p = prompt_layer(x_prompt, wts, cw_k, cw_v, g_mix[0], g_ffn[0], g_final)
    caches = (cache_sb_k[0], cache_sb_v[0], cache_cmp_k[0], cache_cmp_v[0], cache_slc_k[0], cache_slc_v[0],
              state_win_k[0], state_win_v[0])
    y_s, st_s = sample_layer(x_sample, page_table, caches, wts, cw_k, cw_v, g_mix[0], g_ffn[0], g_final)
    return (y_p, y_s, *st_p, *st_s)
```

```python
import functools

import jax
import jax.numpy as jnp
import numpy as np
from jax import lax
from jax.experimental import pallas as pl
from jax.experimental.pallas import tpu as pltpu

F32 = jnp.float32
BF16 = jnp.bfloat16

HEAD_DIM = 128
LANES = 128
NSA_KV_HEADS = 4
NSA_GROUP = 4
CMP_LEN = 32
CMP_STRIDE = 16
CMP_HID = 2 * HEAD_DIM
SLC_LEN = 64
N_SEL = 16
WINDOW = 512
ROPE_THETA = 500000.0
ROPE_DIM = HEAD_DIM // 4
ROPE_HALF = ROPE_DIM // 2
RMS_EPS = 1e-6
NEG = -1e30
PAGE_SIZE = 128
SCALE = HEAD_DIM ** -0.5
VMEM_LIMIT = 56 * 1024 * 1024


def _cparams(sem):
    return pltpu.CompilerParams(dimension_semantics=sem, vmem_limit_bytes=VMEM_LIMIT)


def _dot(a, b):
    return jnp.dot(a, b, preferred_element_type=F32)


def _dot_nt(a, b):
    return lax.dot_general(a, b, (((1,), (1,)), ((), ())), preferred_element_type=F32)


def _dot_split(x, m_bf16):
    hi = x.astype(BF16)
    lo = (x - hi.astype(F32)).astype(BF16)
    return _dot(hi, m_bf16) + _dot(lo, m_bf16)


def _dot_split_left(m_bf16, x):
    hi = x.astype(BF16)
    lo = (x - hi.astype(F32)).astype(BF16)
    return _dot(m_bf16, hi) + _dot(m_bf16, lo)


def _idiv(x, d):
    return x >> (d.bit_length() - 1) if d & (d - 1) == 0 else x // d


def _imod(x, d):
    return x & (d - 1) if d & (d - 1) == 0 else x % d


def _softplus(z):
    return jnp.maximum(z, 0.0) + jnp.log(1.0 + jnp.exp(-jnp.abs(z)))


def _rmsnorm_kernel(x_ref, g_ref, o_ref):
    x = x_ref[...]
    y = x * lax.rsqrt(jnp.mean(x * x, axis=-1, keepdims=True) + RMS_EPS)
    o_ref[...] = (y * g_ref[...]).astype(o_ref.dtype)


def rmsnorm(x, g, out_dtype, tm=256):
    m, d = x.shape
    return pl.pallas_call(
        _rmsnorm_kernel,
        grid=(m // tm,),
        in_specs=[pl.BlockSpec((tm, d), lambda i: (i, 0)),
                  pl.BlockSpec((1, d), lambda i: (0, 0))],
        out_specs=pl.BlockSpec((tm, d), lambda i: (i, 0)),
        out_shape=jax.ShapeDtypeStruct((m, d), out_dtype),
        compiler_params=_cparams(("parallel",)),
        name="rmsnorm",
    )(x, g.reshape(1, d))


def _rope_tile(acc, c, s1, s2):
    return (acc * c + pltpu.roll(acc, LANES - ROPE_HALF, axis=1) * s1
            + pltpu.roll(acc, ROPE_HALF, axis=1) * s2)


def _mm_kernel(*refs, epi):
    if epi == "rope":
        x_ref, w_ref, c_ref, s1_ref, s2_ref, o_ref = refs
    elif epi == "residual":
        x_ref, w_ref, r_ref, o_ref = refs
    else:
        x_ref, w_ref, o_ref = refs
    acc = _dot(x_ref[...], w_ref[...])
    if epi == "rope":
        c, s1, s2 = c_ref[...], s1_ref[...], s2_ref[...]
        for hh in range(acc.shape[1] // LANES):
            sl = slice(hh * LANES, (hh + 1) * LANES)
            o_ref[:, sl] = _rope_tile(acc[:, sl], c, s1, s2).astype(o_ref.dtype)
    elif epi == "sigmoid":
        o_ref[...] = jax.nn.sigmoid(acc).astype(o_ref.dtype)
    elif epi == "residual":
        o_ref[...] = (r_ref[...] + acc).astype(o_ref.dtype)
    else:
        o_ref[...] = acc.astype(o_ref.dtype)


def mm(x, w, out_dtype, epi="plain", extras=(), tm=512, tn=512, table_rows=None):
    m, k = x.shape
    n = w.shape[1]
    tm, tn = min(tm, m), min(tn, n)
    in_specs = [pl.BlockSpec((tm, k), lambda i, j: (i, 0)),
                pl.BlockSpec((k, tn), lambda i, j: (0, j))]
    if epi == "rope":
        nblk = table_rows // tm
        in_specs += [pl.BlockSpec((tm, LANES), lambda i, j: (i % nblk, 0))] * 3
    elif epi == "residual":
        in_specs += [pl.BlockSpec((tm, tn), lambda i, j: (i, j))]
    return pl.pallas_call(
        functools.partial(_mm_kernel, epi=epi),
        grid=(m // tm, n // tn),
        in_specs=in_specs,
        out_specs=pl.BlockSpec((tm, tn), lambda i, j: (i, j)),
        out_shape=jax.ShapeDtypeStruct((m, n), out_dtype),
        compiler_params=_cparams(("parallel", "arbitrary")),
        name="mm_" + epi,
    )(x, w, *extras)


def _merge_kernel(a_ref, b_ref, wa_ref, wb_ref, ga_ref, gb_ref, o_ref):
    ya = _dot(a_ref[...], wa_ref[...])
    yb = _dot(b_ref[...], wb_ref[...])
    o_ref[...] = (jax.nn.sigmoid(ga_ref[...]) * ya + jax.nn.sigmoid(gb_ref[...]) * yb).astype(o_ref.dtype)


def merge(o_a, o_b, wa, wb, g_a, g_b, tm=512, tn=512):
    m, k = o_a.shape
    n = wa.shape[1]
    tm = min(tm, m)
    xs = pl.BlockSpec((tm, k), lambda i, j: (i, 0))
    ws = pl.BlockSpec((k, tn), lambda i, j: (0, j))
    gs = pl.BlockSpec((tm, tn), lambda i, j: (i, j))
    return pl.pallas_call(
        _merge_kernel,
        grid=(m // tm, n // tn),
        in_specs=[xs, xs, ws, ws, gs, gs],
        out_specs=gs,
        out_shape=jax.ShapeDtypeStruct((m, n), BF16),
        compiler_params=_cparams(("parallel", "arbitrary")),
        name="merge",
    )(o_a, o_b, wa, wb, g_a, g_b)


def _ffn_up_kernel(x_ref, wg_ref, wu_ref, o_ref):
    x = x_ref[...]
    a = _dot(x, wg_ref[...])
    b = _dot(x, wu_ref[...])
    o_ref[...] = (a * jax.nn.sigmoid(a) * b).astype(o_ref.dtype)


def ffn_up(x, wg, wu, tm=512, tn=256):
    m, k = x.shape
    n = wg.shape[1]
    tm = min(tm, m)
    ws = pl.BlockSpec((k, tn), lambda i, j: (0, j))
    return pl.pallas_call(
        _ffn_up_kernel,
        grid=(m // tm, n // tn),
        in_specs=[pl.BlockSpec((tm, k), lambda i, j: (i, 0)), ws, ws],
        out_specs=pl.BlockSpec((tm, tn), lambda i, j: (i, j)),
        out_shape=jax.ShapeDtypeStruct((m, n), BF16),
        compiler_params=_cparams(("parallel", "arbitrary")),
        name="ffn_up",
    )(x, wg, wu)


def rope_tables(pos):
    inv = jnp.float32(ROPE_THETA) ** (-(jnp.arange(ROPE_HALF, dtype=F32) * 2.0 / ROPE_DIM))
    ang = pos.astype(F32)[:, None] * inv[None, :]
    cos, sin = jnp.cos(ang), jnp.sin(ang)
    n = pos.shape[0]
    rest = HEAD_DIM - ROPE_DIM
    c = jnp.concatenate([cos, cos, jnp.ones((n, rest), F32)], axis=1)
    s1 = jnp.concatenate([-sin, jnp.zeros((n, HEAD_DIM - ROPE_HALF), F32)], axis=1)
    s2 = jnp.concatenate([jnp.zeros((n, ROPE_HALF), F32), sin, jnp.zeros((n, rest), F32)], axis=1)
    return c, s1, s2


SB_BLK = 256


def _strict_lower(n):
    r = lax.broadcasted_iota(jnp.int32, (n, n), 0)
    c = lax.broadcasted_iota(jnp.int32, (n, n), 1)
    return jnp.where(r > c, 1.0, 0.0).astype(BF16)


def _sb_tile(q, k, v, tri, carry, mask):
    z = _dot_nt(q, k) * SCALE
    sp = _softplus(z)
    lg = -sp if mask is None else jnp.where(mask, -sp, 0.0)
    later = _dot_split(lg, tri) + carry
    a = jnp.exp(z - sp + later)
    if mask is not None:
        a = jnp.where(mask, a, 0.0)
    return _dot(a.astype(BF16), v), carry + jnp.sum(lg, axis=1, keepdims=True)


def _sb_prompt_kernel(q_ref, k_ref, v_ref, o_ref, kb_ref, vb_ref):
    t = q_ref.shape[1]
    kb_ref[...] = k_ref[0].astype(BF16)
    vb_ref[...] = v_ref[0].astype(BF16)
    tri = _strict_lower(SB_BLK)
    r = lax.broadcasted_iota(jnp.int32, (SB_BLK, SB_BLK), 0)
    c = lax.broadcasted_iota(jnp.int32, (SB_BLK, SB_BLK), 1)
    diag_mask = c < r

    def q_block(qi, _):
        q0 = pl.multiple_of(qi * SB_BLK, SB_BLK)
        q = q_ref[0, pl.ds(q0, SB_BLK), :]
        acc, carry = _sb_tile(q, kb_ref[pl.ds(q0, SB_BLK), :], vb_ref[pl.ds(q0, SB_BLK), :], tri,
                              jnp.zeros((SB_BLK, 1), F32), diag_mask)

        def k_block(step, st):
            acc, carry = st
            k0 = pl.multiple_of((qi - 1 - step) * SB_BLK, SB_BLK)
            o, carry = _sb_tile(q, kb_ref[pl.ds(k0, SB_BLK), :], vb_ref[pl.ds(k0, SB_BLK), :], tri, carry, None)
            return acc + o, carry

        acc, _ = lax.fori_loop(0, qi, k_block, (acc, carry))
        o_ref[0, pl.ds(q0, SB_BLK), :] = acc.astype(o_ref.dtype)
        return 0

    lax.fori_loop(0, t // SB_BLK, q_block, 0)


def sb_prompt(q, k, v, n_heads):
    b, t, _ = q.shape
    spec = pl.BlockSpec((1, t, HEAD_DIM), lambda bi, h: (bi, 0, h))
    return pl.pallas_call(
        _sb_prompt_kernel,
        grid=(b, n_heads),
        in_specs=[spec, spec, spec],
        out_specs=spec,
        out_shape=jax.ShapeDtypeStruct(q.shape, BF16),
        scratch_shapes=[pltpu.VMEM((t, HEAD_DIM), BF16), pltpu.VMEM((t, HEAD_DIM), BF16)],
        compiler_params=_cparams(("parallel", "parallel")),
        name="sb_prompt",
    )(q, k, v)


def _split_heads(x_ref, xh_ref, row0):
    n = x_ref.shape[0] // NSA_KV_HEADS
    for h in range(NSA_KV_HEADS):
        xh_ref[h, row0:row0 + n, :] = x_ref[pl.ds(h, n, stride=NSA_KV_HEADS), :]


def _compress_heads(xh_ref, w1_ref, bias_ref, w2_ref, acc_ref):
    n_chunk = xh_ref.shape[1] // CMP_STRIDE
    for sp in range(CMP_STRIDE // 2):
        lhs = jnp.concatenate(
            [jnp.concatenate([xh_ref[h, pl.ds(2 * sp + e, n_chunk, stride=CMP_STRIDE), :].astype(BF16)
                              for e in range(2)], axis=1)
             for h in range(NSA_KV_HEADS)], axis=0)
        part = _dot(lhs, w1_ref[sp])
        if sp == 0:
            acc_ref[...] = part
        else:
            acc_ref[...] += part
    rows = NSA_KV_HEADS * n_chunk
    u0 = acc_ref[:, :CMP_HID]
    u1 = acc_ref[:, CMP_HID:]
    hid = u0 + pltpu.roll(u1, rows - 1, axis=0) + bias_ref[0:1, :]
    hid = hid * jax.nn.sigmoid(hid)
    out = _dot(hid.astype(BF16), w2_ref[...])
    chunk = _imod(lax.broadcasted_iota(jnp.int32, (rows, HEAD_DIM), 0), n_chunk)
    return jnp.where(chunk < n_chunk - 1, out, 0.0)


def _compress_kernel(x_ref, w1_ref, bias_ref, w2_ref, o_ref, xh_ref, acc_ref):
    _split_heads(x_ref.at[0], xh_ref, 0)
    o_ref[0] = _compress_heads(xh_ref, w1_ref, bias_ref, w2_ref, acc_ref).astype(o_ref.dtype)


def compress_prompt(x, w1r, bias, w2):
    b, t, _ = x.shape
    rows = NSA_KV_HEADS * (t // CMP_STRIDE)
    x = x.reshape(b, t * NSA_KV_HEADS, HEAD_DIM)
    return pl.pallas_call(
        _compress_kernel,
        grid=(b,),
        in_specs=[pl.BlockSpec((1, t * NSA_KV_HEADS, HEAD_DIM), lambda i: (i, 0, 0)),
                  pl.BlockSpec(w1r.shape, lambda i: (0, 0, 0)),
                  pl.BlockSpec(bias.shape, lambda i: (0, 0)),
                  pl.BlockSpec(w2.shape, lambda i: (0, 0))],
        out_specs=pl.BlockSpec((1, rows, HEAD_DIM), lambda i: (i, 0, 0)),
        out_shape=jax.ShapeDtypeStruct((b, rows, HEAD_DIM), BF16),
        scratch_shapes=[pltpu.VMEM((NSA_KV_HEADS, t, HEAD_DIM), F32), pltpu.VMEM((rows, 2 * CMP_HID), F32)],
        compiler_params=_cparams(("parallel",)),
        name="compress_prompt",
    )(x, w1r, bias, w2)


def prep_compress_weights(w1, pe, w2):
    r = CMP_LEN // CMP_STRIDE
    w1r = w1.reshape(r, CMP_STRIDE, HEAD_DIM, CMP_HID).transpose(1, 2, 0, 3)
    w1r = w1r.reshape(CMP_STRIDE // 2, 2 * HEAD_DIM, r * CMP_HID).astype(BF16)
    pe_rows = jnp.zeros((16, CMP_LEN * HEAD_DIM), F32).at[0].set(pe.reshape(-1)).astype(BF16)
    bias = mm(pe_rows, w1.reshape(CMP_LEN * HEAD_DIM, CMP_HID).astype(BF16), F32, tn=CMP_HID)
    return w1r, bias, w2.astype(BF16)


def cmp_to_slc_matrix(n_cmp_pad, n_cmp, n_slc):
    i = np.arange(n_cmp_pad)[:, None]
    j = np.arange(LANES)[None, :]
    lo = np.maximum(i * CMP_STRIDE, j * SLC_LEN)
    hi = np.minimum(i * CMP_STRIDE + CMP_LEN, j * SLC_LEN + SLC_LEN)
    m = np.clip(hi - lo, 0, None) / CMP_LEN
    m = np.where((i < n_cmp) & (j < n_slc), m, 0.0)
    return jnp.asarray(m.astype(np.float32)).astype(BF16)


def _masked_softmax(s, mask):
    s = jnp.where(mask, s, NEG)
    m = jnp.max(s, axis=1, keepdims=True)
    e = jnp.where(mask, jnp.exp(s - m), 0.0)
    den = jnp.sum(e, axis=1, keepdims=True)
    return e / jnp.where(den > 0.0, den, 1.0)


def _select_blocks(imp, qpos, n_slc):
    blk = lax.broadcasted_iota(jnp.int32, imp.shape, 1)
    cur = _idiv(qpos, SLC_LEN)
    valid = (blk * SLC_LEN <= qpos) & (blk < n_slc)
    forced = (blk == 0) | (blk == cur) | (blk == cur - 1)
    score = jnp.where(valid, jnp.where(forced, jnp.inf, imp), -jnp.inf)
    rank = jnp.zeros(imp.shape, F32)
    for i in range(n_slc):
        col = score[:, i:i + 1]
        ge = jnp.where(col >= score, 1.0, 0.0)
        gt = jnp.where(col > score, 1.0, 0.0)
        rank = rank + jnp.where(blk > i, ge, gt)
    return jnp.where(valid & (rank < float(min(N_SEL, n_slc))), 1.0, 0.0)


def _select_blocks_t(imp_t, qpos_row, n_slc):
    blk = lax.broadcasted_iota(jnp.int32, imp_t.shape, 0)
    cur = _idiv(qpos_row, SLC_LEN)
    valid = blk * SLC_LEN <= qpos_row
    forced = (blk == 0) | (blk == cur) | (blk == cur - 1)
    score = jnp.where(valid, jnp.where(forced, jnp.inf, imp_t), -jnp.inf)
    rank = jnp.zeros(imp_t.shape, F32)
    for i in range(n_slc):
        row = score[i:i + 1, :]
        ge = jnp.where(row >= score, 1.0, 0.0)
        gt = jnp.where(row > score, 1.0, 0.0)
        rank = rank + jnp.where(blk > i, ge, gt)
    return jnp.where(valid & (rank < float(min(N_SEL, n_slc))), 1.0, 0.0)


NSA_QB = 128
NSA_KB = 256


def _nsa_prompt_kernel(q_ref, ck_ref, cv_ref, ks_ref, vs_ref, kw_ref, vw_ref, g_ref, mt_ref, o_ref,
                       ksb, vsb, kwb, vwb):
    t = ks_ref.shape[1]
    qb = pl.program_id(2)
    n_slc = t // SLC_LEN

    @pl.when(qb == 0)
    def _():
        ksb[...] = ks_ref[0].astype(BF16)
        vsb[...] = vs_ref[0].astype(BF16)
        kwb[...] = kw_ref[0].astype(BF16)
        vwb[...] = vw_ref[0].astype(BF16)

    rows = NSA_GROUP * NSA_QB
    q0 = qb * NSA_QB
    q = jnp.concatenate([q_ref[0, :, g * HEAD_DIM:(g + 1) * HEAD_DIM] for g in range(NSA_GROUP)], axis=0)
    qpos1 = q0 + lax.broadcasted_iota(jnp.int32, (NSA_QB, 1), 0)
    qpos = jnp.concatenate([qpos1] * NSA_GROUP, axis=0)

    n_cmp_pad = ck_ref.shape[2]
    s_c = _dot_nt(q, ck_ref[0, 0]) * SCALE
    cmp_end = lax.broadcasted_iota(jnp.int32, (rows, n_cmp_pad), 1) * CMP_STRIDE + (CMP_LEN - 1)
    p_c = _masked_softmax(s_c, cmp_end <= qpos)
    o_c = _dot(p_c.astype(BF16), cv_ref[0, 0])

    p_sum = p_c[0:NSA_QB]
    for g in range(1, NSA_GROUP):
        p_sum = p_sum + p_c[g * NSA_QB:(g + 1) * NSA_QB]
    p_hi = p_sum.astype(BF16)
    p_lo = (p_sum - p_hi.astype(F32)).astype(BF16)
    imp_t = _dot_nt(mt_ref[...], p_hi) + _dot_nt(mt_ref[...], p_lo)
    qpos_row = q0 + lax.broadcasted_iota(jnp.int32, (1, NSA_QB), 1)
    sel_t = _select_blocks_t(imp_t, qpos_row, n_slc).astype(BF16)
    eye = jnp.where(lax.broadcasted_iota(jnp.int32, (NSA_QB, NSA_QB), 0)
                    == lax.broadcasted_iota(jnp.int32, (NSA_QB, NSA_QB), 1), 1.0, 0.0).astype(BF16)
    sel = _dot_nt(eye, sel_t).astype(BF16)

    def slc_chunk(kc, st):
        m_i, l_i, acc = st
        k0 = pl.multiple_of(kc * NSA_KB, NSA_KB)
        blk_of_key = _idiv(k0 + lax.broadcasted_iota(jnp.int32, (n_slc, NSA_KB), 1), SLC_LEN)
        expand = jnp.where(lax.broadcasted_iota(jnp.int32, (n_slc, NSA_KB), 0) == blk_of_key, 1.0, 0.0)
        chosen = _dot(sel, expand.astype(BF16))
        kpos = k0 + lax.broadcasted_iota(jnp.int32, (NSA_QB, NSA_KB), 1)
        allowed = jnp.where(kpos <= qpos1, chosen, 0.0)
        mask = jnp.concatenate([allowed] * NSA_GROUP, axis=0) > 0.5
        s = jnp.where(mask, _dot_nt(q, ksb[pl.ds(k0, NSA_KB), :]) * SCALE, NEG)
        m_new = jnp.maximum(m_i, jnp.max(s, axis=1, keepdims=True))
        alpha = jnp.exp(m_i - m_new)
        e = jnp.where(mask, jnp.exp(s - m_new), 0.0)
        l_new = alpha * l_i + jnp.sum(e, axis=1, keepdims=True)
        acc = alpha * acc + _dot(e.astype(BF16), vsb[pl.ds(k0, NSA_KB), :])
        return m_new, l_new, acc

    n_chunks = (q0 + NSA_QB + NSA_KB - 1) // NSA_KB
    m_i, l_i, acc = lax.fori_loop(
        0, n_chunks, slc_chunk,
        (jnp.full((rows, 1), NEG, F32), jnp.zeros((rows, 1), F32), jnp.zeros((rows, HEAD_DIM), F32)))
    o_s = acc / jnp.where(l_i > 0.0, l_i, 1.0)

    band = WINDOW + NSA_QB
    w0 = pl.multiple_of(jnp.maximum(q0 - WINDOW, 0), NSA_QB)
    kpos = w0 + lax.broadcasted_iota(jnp.int32, (rows, band), 1)
    dist = qpos - kpos
    s_w = _dot_nt(q, kwb[pl.ds(w0, band), :]) * SCALE
    p_w = _masked_softmax(s_w, (dist >= 0) & (dist < WINDOW))
    o_w = _dot(p_w.astype(BF16), vwb[pl.ds(w0, band), :])

    gates = g_ref[0, 0]
    for g in range(NSA_GROUP):
        rs = slice(g * NSA_QB, (g + 1) * NSA_QB)
        o = (gates[:, 3 * g:3 * g + 1] * o_c[rs] + gates[:, 3 * g + 1:3 * g + 2] * o_s[rs]
             + gates[:, 3 * g + 2:3 * g + 3] * o_w[rs])
        o_ref[0, :, g * HEAD_DIM:(g + 1) * HEAD_DIM] = o.astype(o_ref.dtype)


def nsa_prompt(q, ck, cv, ks, vs, kw, vw, gates, m_mat):
    b, t, _ = q.shape
    m_mat = m_mat[:, :t // SLC_LEN].T
    gw = NSA_GROUP * HEAD_DIM
    qspec = pl.BlockSpec((1, NSA_QB, gw), lambda bi, h, qb: (bi, qb, h))
    cspec = pl.BlockSpec((1, 1, ck.shape[2], HEAD_DIM), lambda bi, h, qb: (bi, h, 0, 0))
    kspec = pl.BlockSpec((1, t, HEAD_DIM), lambda bi, h, qb: (bi, 0, h))
    return pl.pallas_call(
        _nsa_prompt_kernel,
        grid=(b, NSA_KV_HEADS, t // NSA_QB),
        in_specs=[qspec, cspec, cspec, kspec, kspec, kspec, kspec,
                  pl.BlockSpec((1, 1, NSA_QB, LANES), lambda bi, h, qb: (bi, h, qb, 0)),
                  pl.BlockSpec(m_mat.shape, lambda bi, h, qb: (0, 0))],
        out_specs=qspec,
        out_shape=jax.ShapeDtypeStruct(q.shape, BF16),
        scratch_shapes=[pltpu.VMEM((t, HEAD_DIM), BF16)] * 4,
        compiler_params=_cparams(("parallel", "parallel", "arbitrary")),
        name="nsa_prompt",
    )(q, ck, cv, ks, vs, kw, vw, gates, m_mat)


def _diag_heads(acc, n_heads, rows_per_head):
    return jnp.concatenate(
        [acc[h * rows_per_head:(h + 1) * rows_per_head, h * HEAD_DIM:(h + 1) * HEAD_DIM]
         for h in range(n_heads)], axis=0)


def _gather_heads(x_ref, n_heads):
    n = x_ref.shape[0] // n_heads
    return jnp.concatenate([x_ref[pl.ds(h, n, stride=n_heads), :].astype(BF16) for h in range(n_heads)], axis=1)


SB_PAGES_PER_STEP = 4


def _sb_sample_kernel(pt_ref, q_ref, *refs, n_heads, dec_seq):
    g = SB_PAGES_PER_STEP
    kp_refs, vp_refs = refs[:g], refs[g:2 * g]
    kn_ref, vn_ref, o_ref, acc_ref, carry_ref = refs[2 * g:]
    p = pl.program_id(1)
    rows = q_ref.shape[1]
    tri = _strict_lower(PAGE_SIZE)

    def step(k, v, mask):
        o, carry = _sb_tile(q_ref[0], k, v, tri, carry_ref[...], mask)
        acc_ref[...] += o
        carry_ref[...] = carry

    @pl.when(p == 0)
    def _():
        acc_ref[...] = jnp.zeros_like(acc_ref)
        carry_ref[...] = jnp.zeros_like(carry_ref)
        t_row = _imod(lax.broadcasted_iota(jnp.int32, (rows, PAGE_SIZE), 0), dec_seq)
        j = lax.broadcasted_iota(jnp.int32, (rows, PAGE_SIZE), 1)
        step(kn_ref[0], vn_ref[0], j < t_row)

    for i in range(g):
        step(_gather_heads(kp_refs[i].at[0], n_heads), _gather_heads(vp_refs[i].at[0], n_heads), None)

    @pl.when(p == pl.num_programs(1) - 1)
    def _():
        o_ref[0] = _diag_heads(acc_ref[...], n_heads, dec_seq).astype(o_ref.dtype)


def sb_sample(page_table, q_bd, k_pool, v_pool, k_new, v_new, n_heads, dec_seq):
    b, rows, w = q_bd.shape
    n_pages = page_table.shape[1]
    g = SB_PAGES_PER_STEP
    page_rows = PAGE_SIZE * n_heads
    pool_specs = [pl.BlockSpec((1, page_rows, HEAD_DIM),
                               lambda bi, p, pt, i=i: (pt[bi, n_pages - 1 - (p * g + i)], 0, 0)) for i in range(g)]
    new_spec = pl.BlockSpec((1, PAGE_SIZE, w), lambda bi, p, pt: (bi, 0, 0))
    grid_spec = pltpu.PrefetchScalarGridSpec(
        num_scalar_prefetch=1,
        grid=(b, n_pages // g),
        in_specs=[pl.BlockSpec((1, rows, w), lambda bi, p, pt: (bi, 0, 0)),
                  *pool_specs, *pool_specs, new_spec, new_spec],
        out_specs=pl.BlockSpec((1, rows, HEAD_DIM), lambda bi, p, pt: (bi, 0, 0)),
        scratch_shapes=[pltpu.VMEM((rows, w), F32), pltpu.VMEM((rows, 1), F32)],
    )
    return pl.pallas_call(
        functools.partial(_sb_sample_kernel, n_heads=n_heads, dec_seq=dec_seq),
        grid_spec=grid_spec,
        out_shape=jax.ShapeDtypeStruct((b, rows, HEAD_DIM), BF16),
        compiler_params=_cparams(("parallel", "arbitrary")),
        name="sb_sample",
    )(page_table, q_bd, *([k_pool] * g), *([v_pool] * g), k_new, v_new)


def _cmp_sample_kernel(pt_ref, q_ref, *refs, n_pages, dec_seq):
    kp_refs, vp_refs = refs[:n_pages], refs[n_pages:2 * n_pages]
    (w1k_ref, bk_ref, w2k_ref, w1v_ref, bv_ref, w2v_ref, m_ref, oc_ref, sel_ref, xh_ref, acc_ref) = refs[2 * n_pages:]
    past_len = n_pages * PAGE_SIZE
    n_chunk = past_len // CMP_STRIDE
    n_slc = -(-(past_len + dec_seq) // SLC_LEN)

    def compress(page_refs, w1_ref, b_ref, w2_ref):
        for i in range(n_pages):
            _split_heads(page_refs[i].at[0], xh_ref, i * PAGE_SIZE)
        return _compress_heads(xh_ref, w1_ref, b_ref, w2_ref, acc_ref).astype(BF16)

    ck = compress(kp_refs, w1k_ref, bk_ref, w2k_ref)
    cv = compress(vp_refs, w1v_ref, bv_ref, w2v_ref)
    rows = NSA_GROUP * dec_seq
    t_row = _imod(lax.broadcasted_iota(jnp.int32, (rows, 1), 0), dec_seq)
    qpos = past_len + t_row
    cmp_end = lax.broadcasted_iota(jnp.int32, (rows, n_chunk), 1) * CMP_STRIDE + (CMP_LEN - 1)
    r = lax.broadcasted_iota(jnp.int32, (rows, rows), 0)
    c = lax.broadcasted_iota(jnp.int32, (rows, rows), 1)
    same_t = jnp.where(_imod(r, dec_seq) == _imod(c, dec_seq), 1.0, 0.0).astype(BF16)
    for h in range(NSA_KV_HEADS):
        q = q_ref[0, h * rows:(h + 1) * rows, :]
        cs = slice(h * n_chunk, (h + 1) * n_chunk)
        p_c = _masked_softmax(_dot_nt(q, ck[cs]) * SCALE, cmp_end <= qpos)
        oc_ref[0, h * rows:(h + 1) * rows, :] = _dot(p_c.astype(BF16), cv[cs])
        p_sum = _dot_split_left(same_t, p_c)
        imp = _dot_split(p_sum, m_ref[...])
        sel_ref[0, h * rows:(h + 1) * rows, :] = _select_blocks(imp, qpos, n_slc)


def _page_specs(n_pages, page_rows):
    return [pl.BlockSpec((1, page_rows, HEAD_DIM), lambda bi, pt, i=i: (pt[bi, i], 0, 0)) for i in range(n_pages)]


def cmp_sample(page_table, q_rows, k_pool, v_pool, wk, wv, m_mat, dec_seq):
    b, rows, _ = q_rows.shape
    n_pages = page_table.shape[1]
    past_len = n_pages * PAGE_SIZE
    n_chunk = past_len // CMP_STRIDE
    pages = _page_specs(n_pages, PAGE_SIZE * NSA_KV_HEADS)
    const = lambda a: pl.BlockSpec(a.shape, lambda bi, pt: (0,) * a.ndim)
    row_spec = pl.BlockSpec((1, rows, HEAD_DIM), lambda bi, pt: (bi, 0, 0))
    grid_spec = pltpu.PrefetchScalarGridSpec(
        num_scalar_prefetch=1,
        grid=(b,),
        in_specs=[row_spec, *pages, *pages, *[const(a) for a in (*wk, *wv, m_mat)]],
        out_specs=[row_spec, row_spec],
        scratch_shapes=[pltpu.VMEM((NSA_KV_HEADS, past_len, HEAD_DIM), F32),
                        pltpu.VMEM((NSA_KV_HEADS * n_chunk, 2 * CMP_HID), F32)],
    )
    return pl.pallas_call(
        functools.partial(_cmp_sample_kernel, n_pages=n_pages, dec_seq=dec_seq),
        grid_spec=grid_spec,
        out_shape=[jax.ShapeDtypeStruct((b, rows, HEAD_DIM), F32)] * 2,
        compiler_params=_cparams(("parallel",)),
        name="cmp_sample",
    )(page_table, q_rows, *([k_pool] * n_pages), *([v_pool] * n_pages), *wk, *wv, m_mat)


def _nsa_tail_sample_kernel(pt_ref, q_ref, *refs, n_pages, dec_seq):
    kp_refs, vp_refs = refs[:n_pages], refs[n_pages:2 * n_pages]
    (kn_ref, vn_ref, kw_ref, vw_ref, kwn_ref, vwn_ref, oc_ref, sel_ref, g_ref, e_ref, o_ref,
     kall, vall) = refs[2 * n_pages:]
    rows = q_ref.shape[1]
    past_len = n_pages * PAGE_SIZE
    q = q_ref[0]

    for i in range(n_pages):
        kall[i * PAGE_SIZE:(i + 1) * PAGE_SIZE, :] = _gather_heads(kp_refs[i].at[0], NSA_KV_HEADS)
        vall[i * PAGE_SIZE:(i + 1) * PAGE_SIZE, :] = _gather_heads(vp_refs[i].at[0], NSA_KV_HEADS)
    kall[past_len:past_len + PAGE_SIZE, :] = kn_ref[0]
    vall[past_len:past_len + PAGE_SIZE, :] = vn_ref[0]
    n_keys = past_len + PAGE_SIZE
    t_row = _imod(lax.broadcasted_iota(jnp.int32, (rows, n_keys), 0), dec_seq)
    j = lax.broadcasted_iota(jnp.int32, (rows, n_keys), 1)
    chosen = _dot(sel_ref[0].astype(BF16), e_ref[...])
    mask = (chosen > 0.5) & (j - past_len <= t_row)
    p_s = _masked_softmax(_dot_nt(q, kall[...]) * SCALE, mask)
    o_s = _diag_heads(_dot(p_s.astype(BF16), vall[...]), NSA_KV_HEADS, rows // NSA_KV_HEADS)

    k = jnp.concatenate([_gather_heads(kw_ref.at[0], NSA_KV_HEADS), kwn_ref[0]], axis=0)
    v = jnp.concatenate([_gather_heads(vw_ref.at[0], NSA_KV_HEADS), vwn_ref[0]], axis=0)
    n_buf = kw_ref.shape[1] // NSA_KV_HEADS
    n_wkeys = k.shape[0]
    t_row = _imod(lax.broadcasted_iota(jnp.int32, (rows, n_wkeys), 0), dec_seq)
    off = lax.broadcasted_iota(jnp.int32, (rows, n_wkeys), 1) - n_buf
    mask_w = (off <= t_row) & (off > t_row - WINDOW) & (off < dec_seq)
    p_w = _masked_softmax(_dot_nt(q, k) * SCALE, mask_w)
    o_w = _diag_heads(_dot(p_w.astype(BF16), v), NSA_KV_HEADS, rows // NSA_KV_HEADS)

    g = g_ref[0]
    o_ref[0] = (g[:, 0:1] * oc_ref[0] + g[:, 1:2] * o_s + g[:, 2:3] * o_w).astype(o_ref.dtype)


def block_to_key_matrix(n_keys):
    blk = np.arange(LANES)[:, None]
    key = np.arange(n_keys)[None, :]
    return jnp.asarray((blk == key // SLC_LEN).astype(np.float32)).astype(BF16)


def nsa_tail_sample(page_table, q_bd, k_pool, v_pool, k_new, v_new, kw, vw, kw_new, vw_new, o_c, sel, gates, dec_seq):
    b, rows, w = q_bd.shape
    n_pages = page_table.shape[1]
    n_keys = (n_pages + 1) * PAGE_SIZE
    pages = _page_specs(n_pages, PAGE_SIZE * NSA_KV_HEADS)
    e_mat = block_to_key_matrix(n_keys)
    per_b = lambda a: pl.BlockSpec((1, *a.shape[1:]), lambda bi, pt: (bi, 0, 0))
    grid_spec = pltpu.PrefetchScalarGridSpec(
        num_scalar_prefetch=1,
        grid=(b,),
        in_specs=[per_b(q_bd), *pages, *pages,
                  *[per_b(a) for a in (k_new, v_new, kw, vw, kw_new, vw_new, o_c, sel, gates)],
                  pl.BlockSpec(e_mat.shape, lambda bi, pt: (0, 0))],
        out_specs=pl.BlockSpec((1, rows, HEAD_DIM), lambda bi, pt: (bi, 0, 0)),
        scratch_shapes=[pltpu.VMEM((n_keys, w), BF16), pltpu.VMEM((n_keys, w), BF16)],
    )
    return pl.pallas_call(
        functools.partial(_nsa_tail_sample_kernel, n_pages=n_pages, dec_seq=dec_seq),
        grid_spec=grid_spec,
        out_shape=jax.ShapeDtypeStruct((b, rows, HEAD_DIM), BF16),
        compiler_params=_cparams(("parallel",)),
        name="nsa_tail_sample",
    )(page_table, q_bd, *([k_pool] * n_pages), *([v_pool] * n_pages), k_new, v_new, kw, vw, kw_new, vw_new,
      o_c, sel, gates, e_mat)


def prep_weights(w_in, w_branch_a, w_branch_b, w_out, w_ffn_gate, w_ffn_up, w_ffn_down, sb_w, nsa_w, kv_w, d_model):
    n_gate = 3 * (nsa_w // HEAD_DIM)
    widths = (sb_w, sb_w, sb_w, nsa_w, kv_w, kv_w, kv_w, kv_w, kv_w, kv_w, n_gate, d_model, d_model)
    offs = np.concatenate([[0], np.cumsum(widths)])
    parts = [w_in[:, int(offs[i]):int(offs[i + 1])].astype(BF16) for i in range(len(widths))]
    parts[10] = jnp.pad(parts[10], ((0, 0), (0, LANES - n_gate)))
    return dict(w_in=parts, wa=w_branch_a.astype(BF16), wb=w_branch_b.astype(BF16), w_out=w_out.astype(BF16),
                wg=w_ffn_gate.astype(BF16), wu=w_ffn_up.astype(BF16), wd=w_ffn_down.astype(BF16))


def projections(x2d, g_mix, w_parts, tables, table_rows):
    xn = rmsnorm(x2d, g_mix, BF16)
    rope = lambda w, dt: mm(xn, w, dt, epi="rope", extras=tables, table_rows=table_rows)
    plain = lambda w, dt: mm(xn, w, dt)
    q_sb, k_sb, v_sb = plain(w_parts[0], BF16), plain(w_parts[1], F32), plain(w_parts[2], F32)
    q_n = rope(w_parts[3], BF16)
    k_c, v_c = rope(w_parts[4], F32), plain(w_parts[5], F32)
    k_s, v_s = rope(w_parts[6], F32), plain(w_parts[7], F32)
    k_w, v_w = rope(w_parts[8], F32), plain(w_parts[9], F32)
    g_n = mm(xn, w_parts[10], F32, epi="sigmoid")
    g_a, g_b = plain(w_parts[11], F32), plain(w_parts[12], F32)
    return q_sb, k_sb, v_sb, q_n, k_c, v_c, k_s, v_s, k_w, v_w, g_n, g_a, g_b


def merge_and_ffn(x2d, o_sb, o_n, g_a, g_b, wts, g_ffn, g_final):
    mixed = merge(o_sb, o_n, wts["wa"], wts["wb"], g_a, g_b)
    h = mm(mixed, wts["w_out"], F32, epi="residual", extras=(x2d,))
    hn = rmsnorm(h, g_ffn, BF16)
    act = ffn_up(hn, wts["wg"], wts["wu"])
    y = mm(act, wts["wd"], F32, epi="residual", extras=(h,), tn=256)
    return rmsnorm(y, g_final, F32)


def prompt_layer(x, wts, cw_k, cw_v, g_mix, g_ffn, g_final):
    b, t, d = x.shape
    x2d = x.reshape(b * t, d)
    tables = rope_tables(jnp.arange(t))
    (q_sb, k_sb, v_sb, q_n, k_c, v_c, k_s, v_s, k_w, v_w, g_n, g_a, g_b) = projections(
        x2d, g_mix, wts["w_in"], tables, t)
    r3 = lambda a: a.reshape(b, t, -1)
    n_sb_heads = q_sb.shape[1] // HEAD_DIM
    o_sb = sb_prompt(r3(q_sb), r3(k_sb), r3(v_sb), n_sb_heads)

    n_chunk = t // CMP_STRIDE
    ck = compress_prompt(r3(k_c), *cw_k).reshape(b, NSA_KV_HEADS, n_chunk, HEAD_DIM)
    cv = compress_prompt(r3(v_c), *cw_v).reshape(b, NSA_KV_HEADS, n_chunk, HEAD_DIM)
    n_gate = 3 * NSA_GROUP
    gates = g_n[:, :NSA_KV_HEADS * n_gate].reshape(b, t, NSA_KV_HEADS, n_gate).transpose(0, 2, 1, 3)
    gates = jnp.pad(gates, ((0, 0), (0, 0), (0, 0), (0, LANES - n_gate)))
    m_mat = cmp_to_slc_matrix(n_chunk, n_chunk - CMP_LEN // CMP_STRIDE + 1, t // SLC_LEN)
    o_n = nsa_prompt(r3(q_n), ck, cv, r3(k_s), r3(v_s), r3(k_w), r3(v_w), gates, m_mat)

    y = merge_and_ffn(x2d, o_sb.reshape(b * t, -1), o_n.reshape(b * t, -1), g_a, g_b, wts, g_ffn, g_final)
    heads = lambda a: a.reshape(1, b, t, -1, HEAD_DIM)
    keep = min(WINDOW, t)
    states = (heads(k_sb), heads(v_sb), heads(k_c), heads(v_c), heads(k_s), heads(v_s),
              heads(k_w)[:, :, t - keep:], heads(v_w)[:, :, t - keep:])
    return y.reshape(b, t, d), states


def _block_diag_rows(q, n_heads):
    b, r = q.shape[:2]
    qt = q.transpose(0, 2, 1, 3)
    eye = jnp.eye(n_heads, dtype=q.dtype)
    return (qt[:, :, :, None, :] * eye[None, :, None, :, None]).reshape(b, n_heads * r, n_heads * HEAD_DIM)


def _pad_new(a, b, t):
    return jnp.pad(a.reshape(b, t, -1).astype(BF16), ((0, 0), (0, PAGE_SIZE - t), (0, 0)))


def sample_layer(x, page_table, caches, wts, cw_k, cw_v, g_mix, g_ffn, g_final):
    c_sb_k, c_sb_v, c_cmp_k, c_cmp_v, c_slc_k, c_slc_v, w_buf_k, w_buf_v = caches
    b, t, d = x.shape
    n_pages = page_table.shape[1]
    past_len = n_pages * PAGE_SIZE
    x2d = x.reshape(b * t, d)
    tables = rope_tables(past_len + jnp.arange(b * t) % t)
    (q_sb, k_sb, v_sb, q_n, k_c, v_c, k_s, v_s, k_w, v_w, g_n, g_a, g_b) = projections(
        x2d, g_mix, wts["w_in"], tables, b * t)
    pool_rows = lambda a: a.reshape(a.shape[0], -1, HEAD_DIM)

    n_sb_heads = q_sb.shape[1] // HEAD_DIM
    q_bd = _block_diag_rows(q_sb.reshape(b, t, n_sb_heads, HEAD_DIM), n_sb_heads)
    o_sb = sb_sample(page_table, q_bd, pool_rows(c_sb_k), pool_rows(c_sb_v), _pad_new(k_sb, b, t),
                     _pad_new(v_sb, b, t), n_sb_heads, t)
    o_sb = o_sb.reshape(b, n_sb_heads, t, HEAD_DIM).transpose(0, 2, 1, 3).reshape(b * t, -1)

    qn5 = q_n.reshape(b, t, NSA_KV_HEADS, NSA_GROUP, HEAD_DIM).transpose(0, 2, 3, 1, 4)
    rows = NSA_KV_HEADS * NSA_GROUP * t
    q_rows = qn5.reshape(b, rows, HEAD_DIM)
    qn_bd = _block_diag_rows(qn5.reshape(b, NSA_KV_HEADS, NSA_GROUP * t, HEAD_DIM).transpose(0, 2, 1, 3), NSA_KV_HEADS)
    n_chunk = past_len // CMP_STRIDE
    n_slc = -(-(past_len + t) // SLC_LEN)
    m_mat = cmp_to_slc_matrix(n_chunk, n_chunk - CMP_LEN // CMP_STRIDE + 1, n_slc)
    o_c, sel = cmp_sample(page_table, q_rows, pool_rows(c_cmp_k), pool_rows(c_cmp_v), cw_k, cw_v, m_mat, t)
    gates = g_n[:, :rows // t * 3].reshape(b, t, NSA_KV_HEADS, NSA_GROUP, 3).transpose(0, 2, 3, 1, 4)
    gates = jnp.pad(gates.reshape(b, rows, 3), ((0, 0), (0, 0), (0, LANES - 3)))
    o_n = nsa_tail_sample(page_table, qn_bd, pool_rows(c_slc_k), pool_rows(c_slc_v), _pad_new(k_s, b, t),
                          _pad_new(v_s, b, t), pool_rows(w_buf_k), pool_rows(w_buf_v), _pad_new(k_w, b, t),
                          _pad_new(v_w, b, t), o_c, sel, gates, t)
    o_n = o_n.reshape(b, NSA_KV_HEADS, NSA_GROUP, t, HEAD_DIM).transpose(0, 3, 1, 2, 4).reshape(b * t, -1)

    y = merge_and_ffn(x2d, o_sb, o_n, g_a, g_b, wts, g_ffn, g_final)
    heads = lambda a: a.reshape(1, b, t, -1, HEAD_DIM)
    wk = jnp.concatenate([w_buf_k[None], heads(k_w)], axis=2)[:, :, t:]
    wv = jnp.concatenate([w_buf_v[None], heads(v_w)], axis=2)[:, :, t:]
    states = (heads(k_sb), heads(v_sb), heads(k_c), heads(v_c), heads(k_s), heads(v_s), wk, wv)
    return y.reshape(b, t, d), states


def kernel(x_prompt, x_sample, cache_sb_k, cache_sb_v, cache_cmp_k, cache_cmp_v, cache_slc_k, cache_slc_v,
           state_win_k, state_win_v, page_table, g_mix, w_in, w_cmp_k1, w_cmp_k2, pe_cmp_k, w_cmp_v1, w_cmp_v2,
           pe_cmp_v, w_branch_a, w_branch_b, w_out, g_ffn, w_ffn_gate, w_ffn_up, w_ffn_down, g_final):
    assert w_in.shape[0] == 1, "single-layer trunk"
    d_model = x_prompt.shape[-1]
    sb_w = cache_sb_k.shape[3] * HEAD_DIM
    kv_w = NSA_KV_HEADS * HEAD_DIM
    nsa_w = NSA_KV_HEADS * NSA_GROUP * HEAD_DIM
    wts = prep_weights(w_in[0], w_branch_a[0], w_branch_b[0], w_out[0], w_ffn_gate[0], w_ffn_up[0],
                       w_ffn_down[0], sb_w, nsa_w, kv_w, d_model)
    cw_k = prep_compress_weights(w_cmp_k1[0], pe_cmp_k[0], w_cmp_k2[0])
    cw_v = prep_compress_weights(w_cmp_v1[0], pe_cmp_v[0], w_cmp_v2[0])
    y_p, st_p = prompt_layer(x_prompt, wts, cw_k, cw_v, g_mix[0], g_ffn[0], g_final)
    caches = (cache_sb_k[0], cache_sb_v[0], cache_cmp_k[0], cache_cmp_v[0], cache_slc_k[0], cache_slc_v[0],
              state_win_k[0], state_win_v[0])
    y_s, st_s = sample_layer(x_sample, page_table, caches, wts, cw_k, cw_v, g_mix[0], g_ffn[0], g_final)
    return (y_p, y_s, *st_p, *st_s)
```

```python
import functools

import jax
import jax.numpy as jnp
import numpy as np
from jax import lax
from jax.experimental import pallas as pl
from jax.experimental.pallas import tpu as pltpu

F32 = jnp.float32
BF16 = jnp.bfloat16

HEAD_DIM = 128
LANES = 128
NSA_KV_HEADS = 4
NSA_GROUP = 4
CMP_LEN = 32
CMP_STRIDE = 16
CMP_HID = 2 * HEAD_DIM
SLC_LEN = 64
N_SEL = 16
WINDOW = 512
ROPE_THETA = 500000.0
ROPE_DIM = HEAD_DIM // 4
ROPE_HALF = ROPE_DIM // 2
RMS_EPS = 1e-6
NEG = -1e30
PAGE_SIZE = 128
SCALE = HEAD_DIM ** -0.5
LOG2E = 1.4426950408889634
VMEM_LIMIT = 56 * 1024 * 1024


def _cparams(sem):
    return pltpu.CompilerParams(dimension_semantics=sem, vmem_limit_bytes=VMEM_LIMIT)


def _dot(a, b):
    return jnp.dot(a, b, preferred_element_type=F32)


def _dot_nt(a, b):
    return lax.dot_general(a, b, (((1,), (1,)), ((), ())), preferred_element_type=F32)


def _dot_split(x, m_bf16):
    hi = x.astype(BF16)
    lo = (x - hi.astype(F32)).astype(BF16)
    return _dot(hi, m_bf16) + _dot(lo, m_bf16)


def _dot_split_left(m_bf16, x):
    hi = x.astype(BF16)
    lo = (x - hi.astype(F32)).astype(BF16)
    return _dot(m_bf16, hi) + _dot(m_bf16, lo)


def _idiv(x, d):
    return x >> (d.bit_length() - 1) if d & (d - 1) == 0 else x // d


def _imod(x, d):
    return x & (d - 1) if d & (d - 1) == 0 else x % d


def _softplus(z):
    return jnp.maximum(z, 0.0) + jnp.log(1.0 + jnp.exp(-jnp.abs(z)))


def _rmsnorm_kernel(x_ref, g_ref, o_ref):
    x = x_ref[...]
    y = x * lax.rsqrt(jnp.mean(x * x, axis=-1, keepdims=True) + RMS_EPS)
    o_ref[...] = (y * g_ref[...]).astype(o_ref.dtype)


def rmsnorm(x, g, out_dtype, tm=256):
    m, d = x.shape
    return pl.pallas_call(
        _rmsnorm_kernel,
        grid=(m // tm,),
        in_specs=[pl.BlockSpec((tm, d), lambda i: (i, 0)),
                  pl.BlockSpec((1, d), lambda i: (0, 0))],
        out_specs=pl.BlockSpec((tm, d), lambda i: (i, 0)),
        out_shape=jax.ShapeDtypeStruct((m, d), out_dtype),
        compiler_params=_cparams(("parallel",)),
        name="rmsnorm",
    )(x, g.reshape(1, d))


def _rope_tile(acc, c, s1, s2):
    return (acc * c + pltpu.roll(acc, LANES - ROPE_HALF, axis=1) * s1
            + pltpu.roll(acc, ROPE_HALF, axis=1) * s2)


def _mm_kernel(*refs, epi):
    if epi == "rope":
        x_ref, w_ref, c_ref, s1_ref, s2_ref, o_ref = refs
    elif epi == "residual":
        x_ref, w_ref, r_ref, o_ref = refs
    else:
        x_ref, w_ref, o_ref = refs
    acc = _dot(x_ref[...], w_ref[...])
    if epi == "rope":
        c, s1, s2 = c_ref[...], s1_ref[...], s2_ref[...]
        for hh in range(acc.shape[1] // LANES):
            sl = slice(hh * LANES, (hh + 1) * LANES)
            o_ref[:, sl] = _rope_tile(acc[:, sl], c, s1, s2).astype(o_ref.dtype)
    elif epi == "sigmoid":
        o_ref[...] = jax.nn.sigmoid(acc).astype(o_ref.dtype)
    elif epi == "residual":
        o_ref[...] = (r_ref[...] + acc).astype(o_ref.dtype)
    else:
        o_ref[...] = acc.astype(o_ref.dtype)


def mm(x, w, out_dtype, epi="plain", extras=(), tm=1024, tn=512, table_rows=None):
    m, k = x.shape
    n = w.shape[1]
    tm, tn = min(tm, m), min(tn, n)
    in_specs = [pl.BlockSpec((tm, k), lambda i, j: (i, 0)),
                pl.BlockSpec((k, tn), lambda i, j: (0, j))]
    if epi == "rope":
        nblk = table_rows // tm
        in_specs += [pl.BlockSpec((tm, LANES), lambda i, j: (i % nblk, 0))] * 3
    elif epi == "residual":
        in_specs += [pl.BlockSpec((tm, tn), lambda i, j: (i, j))]
    return pl.pallas_call(
        functools.partial(_mm_kernel, epi=epi),
        grid=(m // tm, n // tn),
        in_specs=in_specs,
        out_specs=pl.BlockSpec((tm, tn), lambda i, j: (i, j)),
        out_shape=jax.ShapeDtypeStruct((m, n), out_dtype),
        compiler_params=_cparams(("parallel", "arbitrary")),
        name="mm_" + epi,
    )(x, w, *extras)


def _merge_kernel(a_ref, b_ref, wa_ref, wb_ref, ga_ref, gb_ref, o_ref):
    ya = _dot(a_ref[...], wa_ref[...])
    yb = _dot(b_ref[...], wb_ref[...])
    o_ref[...] = (jax.nn.sigmoid(ga_ref[...]) * ya + jax.nn.sigmoid(gb_ref[...]) * yb).astype(o_ref.dtype)


def merge(o_a, o_b, wa, wb, g_a, g_b, tm=1024, tn=512):
    m, k = o_a.shape
    n = wa.shape[1]
    tm = min(tm, m)
    xs = pl.BlockSpec((tm, k), lambda i, j: (i, 0))
    ws = pl.BlockSpec((k, tn), lambda i, j: (0, j))
    gs = pl.BlockSpec((tm, tn), lambda i, j: (i, j))
    return pl.pallas_call(
        _merge_kernel,
        grid=(m // tm, n // tn),
        in_specs=[xs, xs, ws, ws, gs, gs],
        out_specs=gs,
        out_shape=jax.ShapeDtypeStruct((m, n), BF16),
        compiler_params=_cparams(("parallel", "arbitrary")),
        name="merge",
    )(o_a, o_b, wa, wb, g_a, g_b)


def _ffn_up_kernel(x_ref, wg_ref, wu_ref, o_ref):
    x = x_ref[...]
    a = _dot(x, wg_ref[...])
    b = _dot(x, wu_ref[...])
    o_ref[...] = (a * jax.nn.sigmoid(a) * b).astype(o_ref.dtype)


def ffn_up(x, wg, wu, tm=1024, tn=256):
    m, k = x.shape
    n = wg.shape[1]
    tm = min(tm, m)
    ws = pl.BlockSpec((k, tn), lambda i, j: (0, j))
    return pl.pallas_call(
        _ffn_up_kernel,
        grid=(m // tm, n // tn),
        in_specs=[pl.BlockSpec((tm, k), lambda i, j: (i, 0)), ws, ws],
        out_specs=pl.BlockSpec((tm, tn), lambda i, j: (i, j)),
        out_shape=jax.ShapeDtypeStruct((m, n), BF16),
        compiler_params=_cparams(("parallel", "arbitrary")),
        name="ffn_up",
    )(x, wg, wu)


def rope_tables(pos):
    inv = jnp.float32(ROPE_THETA) ** (-(jnp.arange(ROPE_HALF, dtype=F32) * 2.0 / ROPE_DIM))
    ang = pos.astype(F32)[:, None] * inv[None, :]
    cos, sin = jnp.cos(ang), jnp.sin(ang)
    n = pos.shape[0]
    rest = HEAD_DIM - ROPE_DIM
    c = jnp.concatenate([cos, cos, jnp.ones((n, rest), F32)], axis=1)
    s1 = jnp.concatenate([-sin, jnp.zeros((n, HEAD_DIM - ROPE_HALF), F32)], axis=1)
    s2 = jnp.concatenate([jnp.zeros((n, ROPE_HALF), F32), sin, jnp.zeros((n, rest), F32)], axis=1)
    return c, s1, s2


SB_BLK = 256
SB_HEADS_PER_STEP = 4


def _strict_lower(n):
    r = lax.broadcasted_iota(jnp.int32, (n, n), 0)
    c = lax.broadcasted_iota(jnp.int32, (n, n), 1)
    return jnp.where(r > c, 1.0, 0.0).astype(BF16)


def _sb_tiles(qs, ks, vs, tri, carry_in, mask, chained, expand=None):
    zs = [_dot_nt(q, k) * SCALE for q, k in zip(qs, ks)]
    pre, lgs, his, los = [], [], [], []
    for z in zs:
        sp = _softplus(z)
        lg = -sp if mask is None else jnp.where(mask, -sp, 0.0)
        hi = lg.astype(BF16)
        pre.append(z - sp)
        lgs.append(lg)
        his.append(hi)
        los.append((lg - hi.astype(F32)).astype(BF16))
    cs_hi = [_dot(hi, tri) for hi in his]
    cs_lo = [_dot(lo, tri) for lo in los]
    carries, probs = [], []
    carry = carry_in
    for i, lg in enumerate(lgs):
        c0 = carry if chained else carry_in[i]
        a = jnp.exp(pre[i] + (cs_hi[i] + cs_lo[i]) + c0)
        if mask is not None:
            a = jnp.where(mask, a, 0.0)
        probs.append(a.astype(BF16))
        carry = c0 + jnp.sum(lg, axis=1, keepdims=True)
        carries.append(carry)
    if expand is not None:
        spread, own_head = expand
        probs = [_dot(a, spread).astype(BF16) * own_head for a in probs]
    return [_dot(a, v) for a, v in zip(probs, vs)], carries


def _sb_tile(q, k, v, tri, carry, mask):
    outs, carries = _sb_tiles([q], [k], [v], tri, [carry], mask, False)
    return outs[0], carries[0]


def _sb_prompt_kernel(q_ref, k_ref, v_ref, o_ref, kb_ref, vb_ref):
    t = q_ref.shape[1]
    kb_ref[...] = k_ref[0].astype(BF16)
    vb_ref[...] = v_ref[0].astype(BF16)
    tri = _strict_lower(SB_BLK)
    r = lax.broadcasted_iota(jnp.int32, (SB_BLK, SB_BLK), 0)
    c = lax.broadcasted_iota(jnp.int32, (SB_BLK, SB_BLK), 1)
    diag_mask = c < r
    heads = [slice(h * HEAD_DIM, (h + 1) * HEAD_DIM) for h in range(SB_HEADS_PER_STEP)]

    def q_block(qi, _):
        q0 = pl.multiple_of(qi * SB_BLK, SB_BLK)
        qs = [q_ref[0, pl.ds(q0, SB_BLK), hs] for hs in heads]

        def tiles(k0, carries, mask):
            return _sb_tiles(qs, [kb_ref[pl.ds(k0, SB_BLK), hs] for hs in heads],
                             [vb_ref[pl.ds(k0, SB_BLK), hs] for hs in heads], tri, carries, mask, False)

        accs, carries = tiles(q0, [jnp.zeros((SB_BLK, 1), F32)] * len(heads), diag_mask)

        def k_block(step, st):
            accs, carries = st
            outs, carries = tiles(pl.multiple_of((qi - 1 - step) * SB_BLK, SB_BLK), carries, None)
            return tuple(a + o for a, o in zip(accs, outs)), tuple(carries)

        accs, _ = lax.fori_loop(0, qi, k_block, (tuple(accs), tuple(carries)))
        for hs, acc in zip(heads, accs):
            o_ref[0, pl.ds(q0, SB_BLK), hs] = acc.astype(o_ref.dtype)
        return 0

    lax.fori_loop(0, t // SB_BLK, q_block, 0)


def sb_prompt(q, k, v, n_heads):
    b, t, _ = q.shape
    w = SB_HEADS_PER_STEP * HEAD_DIM
    spec = pl.BlockSpec((1, t, w), lambda bi, h: (bi, 0, h))
    return pl.pallas_call(
        _sb_prompt_kernel,
        grid=(b, n_heads // SB_HEADS_PER_STEP),
        in_specs=[spec, spec, spec],
        out_specs=spec,
        out_shape=jax.ShapeDtypeStruct(q.shape, BF16),
        scratch_shapes=[pltpu.VMEM((t, w), BF16), pltpu.VMEM((t, w), BF16)],
        compiler_params=_cparams(("parallel", "parallel")),
        name="sb_prompt",
    )(q, k, v)


def _split_heads(x_ref, xh_ref, row0):
    n = x_ref.shape[0] // NSA_KV_HEADS
    for h in range(NSA_KV_HEADS):
        xh_ref[h, row0:row0 + n, :] = x_ref[pl.ds(h, n, stride=NSA_KV_HEADS), :]


def _compress_heads(xh_ref, w1_ref, bias_ref, w2_ref, acc_ref):
    n_chunk = xh_ref.shape[1] // CMP_STRIDE
    for sp in range(CMP_STRIDE // 2):
        lhs = jnp.concatenate(
            [jnp.concatenate([xh_ref[h, pl.ds(2 * sp + e, n_chunk, stride=CMP_STRIDE), :].astype(BF16)
                              for e in range(2)], axis=1)
             for h in range(NSA_KV_HEADS)], axis=0)
        part = _dot(lhs, w1_ref[sp])
        if sp == 0:
            acc_ref[...] = part
        else:
            acc_ref[...] += part
    rows = NSA_KV_HEADS * n_chunk
    u0 = acc_ref[:, :CMP_HID]
    u1 = acc_ref[:, CMP_HID:]
    hid = u0 + pltpu.roll(u1, rows - 1, axis=0) + bias_ref[0:1, :]
    hid = hid * jax.nn.sigmoid(hid)
    out = _dot(hid.astype(BF16), w2_ref[...])
    chunk = _imod(lax.broadcasted_iota(jnp.int32, (rows, HEAD_DIM), 0), n_chunk)
    return jnp.where(chunk < n_chunk - 1, out, 0.0)


def _compress_kernel(x_ref, w1_ref, bias_ref, w2_ref, o_ref, xh_ref, acc_ref):
    _split_heads(x_ref.at[0], xh_ref, 0)
    o_ref[0] = _compress_heads(xh_ref, w1_ref, bias_ref, w2_ref, acc_ref).astype(o_ref.dtype)


def compress_prompt(x, w1r, bias, w2):
    b, t, _ = x.shape
    rows = NSA_KV_HEADS * (t // CMP_STRIDE)
    x = x.reshape(b, t * NSA_KV_HEADS, HEAD_DIM)
    return pl.pallas_call(
        _compress_kernel,
        grid=(b,),
        in_specs=[pl.BlockSpec((1, t * NSA_KV_HEADS, HEAD_DIM), lambda i: (i, 0, 0)),
                  pl.BlockSpec(w1r.shape, lambda i: (0, 0, 0)),
                  pl.BlockSpec(bias.shape, lambda i: (0, 0)),
                  pl.BlockSpec(w2.shape, lambda i: (0, 0))],
        out_specs=pl.BlockSpec((1, rows, HEAD_DIM), lambda i: (i, 0, 0)),
        out_shape=jax.ShapeDtypeStruct((b, rows, HEAD_DIM), BF16),
        scratch_shapes=[pltpu.VMEM((NSA_KV_HEADS, t, HEAD_DIM), F32), pltpu.VMEM((rows, 2 * CMP_HID), F32)],
        compiler_params=_cparams(("parallel",)),
        name="compress_prompt",
    )(x, w1r, bias, w2)


def prep_compress_weights(w1, pe, w2):
    r = CMP_LEN // CMP_STRIDE
    w1r = w1.reshape(r, CMP_STRIDE, HEAD_DIM, CMP_HID).transpose(1, 2, 0, 3)
    w1r = w1r.reshape(CMP_STRIDE // 2, 2 * HEAD_DIM, r * CMP_HID).astype(BF16)
    pe_rows = jnp.zeros((16, CMP_LEN * HEAD_DIM), F32).at[0].set(pe.reshape(-1)).astype(BF16)
    bias = mm(pe_rows, w1.reshape(CMP_LEN * HEAD_DIM, CMP_HID).astype(BF16), F32, tn=CMP_HID)
    return w1r, bias, w2.astype(BF16)


def cmp_to_slc_matrix(n_cmp_pad, n_cmp, n_slc):
    i = np.arange(n_cmp_pad)[:, None]
    j = np.arange(LANES)[None, :]
    lo = np.maximum(i * CMP_STRIDE, j * SLC_LEN)
    hi = np.minimum(i * CMP_STRIDE + CMP_LEN, j * SLC_LEN + SLC_LEN)
    m = np.clip(hi - lo, 0, None) / CMP_LEN
    m = np.where((i < n_cmp) & (j < n_slc), m, 0.0)
    return jnp.asarray(m.astype(np.float32)).astype(BF16)


def _masked_softmax(s, mask):
    s = jnp.where(mask, s, NEG)
    m = jnp.max(s, axis=1, keepdims=True)
    e = jnp.where(mask, jnp.exp(s - m), 0.0)
    den = jnp.sum(e, axis=1, keepdims=True)
    return e / jnp.where(den > 0.0, den, 1.0)


def _select_blocks(imp, qpos, n_slc):
    blk = lax.broadcasted_iota(jnp.int32, imp.shape, 1)
    cur = _idiv(qpos, SLC_LEN)
    valid = (blk * SLC_LEN <= qpos) & (blk < n_slc)
    forced = (blk == 0) | (blk == cur) | (blk == cur - 1)
    score = jnp.where(valid, jnp.where(forced, jnp.inf, imp), -jnp.inf)
    rank = jnp.zeros(imp.shape, F32)
    for i in range(n_slc):
        col = score[:, i:i + 1]
        ge = jnp.where(col >= score, 1.0, 0.0)
        gt = jnp.where(col > score, 1.0, 0.0)
        rank = rank + jnp.where(blk > i, ge, gt)
    return jnp.where(valid & (rank < float(min(N_SEL, n_slc))), 1.0, 0.0)


def _select_blocks_t(imp_t, qpos_row, n_slc):
    blk = lax.broadcasted_iota(jnp.int32, imp_t.shape, 0)
    cur = _idiv(qpos_row, SLC_LEN)
    valid = blk * SLC_LEN <= qpos_row
    forced = (blk == 0) | (blk == cur) | (blk == cur - 1)
    score = jnp.where(valid, jnp.where(forced, jnp.inf, imp_t), -jnp.inf)
    rank = jnp.zeros(imp_t.shape, F32)
    for i in range(n_slc):
        row = score[i:i + 1, :]
        ge = jnp.where(row >= score, 1.0, 0.0)
        gt = jnp.where(row > score, 1.0, 0.0)
        rank = rank + jnp.where(blk > i, ge, gt)
    return jnp.where(valid & (rank < float(min(N_SEL, n_slc))), 1.0, 0.0)


NSA_QB = 128
NSA_KB = 512


def _nsa_prompt_kernel(q_ref, ck_ref, cv_ref, ks_ref, vs_ref, kw_ref, vw_ref, g_ref, mt_ref, o_ref,
                       ksb, vsb, kwb, vwb):
    t = ks_ref.shape[1]
    qb = pl.program_id(2)
    n_slc = t // SLC_LEN

    @pl.when(qb == 0)
    def _():
        ksb[...] = ks_ref[0].astype(BF16)
        vsb[...] = vs_ref[0].astype(BF16)
        kwb[...] = kw_ref[0].astype(BF16)
        vwb[...] = vw_ref[0].astype(BF16)

    rows = NSA_GROUP * NSA_QB
    q0 = qb * NSA_QB
    q = jnp.concatenate([q_ref[0, :, g * HEAD_DIM:(g + 1) * HEAD_DIM] for g in range(NSA_GROUP)], axis=0)
    qpos1 = q0 + lax.broadcasted_iota(jnp.int32, (NSA_QB, 1), 0)
    qpos = jnp.concatenate([qpos1] * NSA_GROUP, axis=0)

    tile_rows = lambda a: jnp.concatenate([a] * NSA_GROUP, axis=0)

    n_cmp_pad = ck_ref.shape[2]
    band = WINDOW + NSA_QB
    w0 = pl.multiple_of(jnp.maximum(q0 - WINDOW, 0), NSA_QB)
    s_c = _dot_nt(q, ck_ref[0, 0]) * (SCALE * LOG2E)
    s_w = _dot_nt(q, kwb[pl.ds(w0, band), :]) * (SCALE * LOG2E)

    cmp_end = lax.broadcasted_iota(jnp.int32, (rows, n_cmp_pad), 1) * CMP_STRIDE + (CMP_LEN - 1)
    mask_c = cmp_end <= qpos
    s_c = jnp.where(mask_c, s_c, NEG)
    e_c = jnp.where(mask_c, jnp.exp2(s_c - jnp.max(s_c, axis=1, keepdims=True)), 0.0)
    den_c = jnp.sum(e_c, axis=1, keepdims=True)
    p_c = e_c * (1.0 / jnp.where(den_c > 0.0, den_c, 1.0))
    o_c = _dot(p_c.astype(BF16), cv_ref[0, 0])

    dist = qpos1 - (w0 + lax.broadcasted_iota(jnp.int32, (NSA_QB, band), 1))
    s_w = s_w + tile_rows(jnp.where((dist >= 0) & (dist < WINDOW), 0.0, NEG))
    e_w = jnp.exp2(s_w - jnp.max(s_w, axis=1, keepdims=True))
    p_w = e_w * (1.0 / jnp.sum(e_w, axis=1, keepdims=True))
    o_w = _dot(p_w.astype(BF16), vwb[pl.ds(w0, band), :])

    p_sum = p_c[0:NSA_QB]
    for g in range(1, NSA_GROUP):
        p_sum = p_sum + p_c[g * NSA_QB:(g + 1) * NSA_QB]
    p_hi = p_sum.astype(BF16)
    p_lo = (p_sum - p_hi.astype(F32)).astype(BF16)
    imp_t = _dot_nt(mt_ref[...], p_hi) + _dot_nt(mt_ref[...], p_lo)
    qpos_row = q0 + lax.broadcasted_iota(jnp.int32, (1, NSA_QB), 1)
    sel_t = _select_blocks_t(imp_t, qpos_row, n_slc).astype(BF16)
    eye = jnp.where(lax.broadcasted_iota(jnp.int32, (NSA_QB, NSA_QB), 0)
                    == lax.broadcasted_iota(jnp.int32, (NSA_QB, NSA_QB), 1), 1.0, 0.0).astype(BF16)
    sel = _dot_nt(eye, sel_t).astype(BF16)

    def slc_chunk(kc, st):
        m_i, l_i, acc = st
        k0 = pl.multiple_of(kc * NSA_KB, NSA_KB)
        blk_of_key = _idiv(k0 + lax.broadcasted_iota(jnp.int32, (n_slc, NSA_KB), 1), SLC_LEN)
        expand = jnp.where(lax.broadcasted_iota(jnp.int32, (n_slc, NSA_KB), 0) == blk_of_key, 1.0, 0.0)
        chosen = _dot(sel, expand.astype(BF16))
        kpos = k0 + lax.broadcasted_iota(jnp.int32, (NSA_QB, NSA_KB), 1)
        bias = tile_rows(jnp.where((chosen > 0.5) & (kpos <= qpos1), 0.0, NEG))
        s = _dot_nt(q, ksb[pl.ds(k0, NSA_KB), :]) * (SCALE * LOG2E) + bias
        m_new = jnp.maximum(m_i, jnp.max(s, axis=1, keepdims=True))
        alpha = jnp.exp2(m_i - m_new)
        e = jnp.exp2(s - m_new)
        l_new = alpha * l_i + jnp.sum(e, axis=1, keepdims=True)
        acc = alpha * acc + _dot(e.astype(BF16), vsb[pl.ds(k0, NSA_KB), :])
        return m_new, l_new, acc

    n_chunks = (q0 + NSA_QB + NSA_KB - 1) // NSA_KB
    m_i, l_i, acc = lax.fori_loop(
        0, n_chunks, slc_chunk,
        (jnp.full((rows, 1), NEG, F32), jnp.zeros((rows, 1), F32), jnp.zeros((rows, HEAD_DIM), F32)))
    o_s = acc * (1.0 / l_i)

    gates = g_ref[0, 0]
    for g in range(NSA_GROUP):
        rs = slice(g * NSA_QB, (g + 1) * NSA_QB)
        o = (gates[:, 3 * g:3 * g + 1] * o_c[rs] + gates[:, 3 * g + 1:3 * g + 2] * o_s[rs]
             + gates[:, 3 * g + 2:3 * g + 3] * o_w[rs])
        o_ref[0, :, g * HEAD_DIM:(g + 1) * HEAD_DIM] = o.astype(o_ref.dtype)


def nsa_prompt(q, ck, cv, ks, vs, kw, vw, gates, m_mat):
    b, t, _ = q.shape
    m_mat = m_mat[:, :t // SLC_LEN].T
    gw = NSA_GROUP * HEAD_DIM
    qspec = pl.BlockSpec((1, NSA_QB, gw), lambda bi, h, qb: (bi, qb, h))
    cspec = pl.BlockSpec((1, 1, ck.shape[2], HEAD_DIM), lambda bi, h, qb: (bi, h, 0, 0))
    kspec = pl.BlockSpec((1, t, HEAD_DIM), lambda bi, h, qb: (bi, 0, h))
    return pl.pallas_call(
        _nsa_prompt_kernel,
        grid=(b, NSA_KV_HEADS, t // NSA_QB),
        in_specs=[qspec, cspec, cspec, kspec, kspec, kspec, kspec,
                  pl.BlockSpec((1, 1, NSA_QB, LANES), lambda bi, h, qb: (bi, h, qb, 0)),
                  pl.BlockSpec(m_mat.shape, lambda bi, h, qb: (0, 0))],
        out_specs=qspec,
        out_shape=jax.ShapeDtypeStruct(q.shape, BF16),
        scratch_shapes=[pltpu.VMEM((t, HEAD_DIM), BF16)] * 4,
        compiler_params=_cparams(("parallel", "parallel", "arbitrary")),
        name="nsa_prompt",
    )(q, ck, cv, ks, vs, kw, vw, gates, m_mat)


def _diag_heads(acc, n_heads, rows_per_head):
    return jnp.concatenate(
        [acc[h * rows_per_head:(h + 1) * rows_per_head, h * HEAD_DIM:(h + 1) * HEAD_DIM]
         for h in range(n_heads)], axis=0)


def _gather_heads(x_ref, n_heads):
    n = x_ref.shape[0] // n_heads
    return jnp.concatenate([x_ref[pl.ds(h, n, stride=n_heads), :].astype(BF16) for h in range(n_heads)], axis=1)


SB_PAGES_PER_STEP = 4


def _sb_sample_kernel(pt_ref, q_ref, *refs, n_heads, dec_seq):
    g = SB_PAGES_PER_STEP
    kp_refs, vp_refs = refs[:g], refs[g:2 * g]
    kn_ref, vn_ref, spread_ref, own_ref, o_ref, acc_ref, carry_ref = refs[2 * g:]
    p = pl.program_id(1)
    rows = q_ref.shape[1]
    tri = _strict_lower(PAGE_SIZE)

    @pl.when(p == 0)
    def _():
        t_row = _imod(lax.broadcasted_iota(jnp.int32, (rows, PAGE_SIZE), 0), dec_seq)
        j = lax.broadcasted_iota(jnp.int32, (rows, PAGE_SIZE), 1)
        o, carry = _sb_tile(q_ref[0], kn_ref[0], vn_ref[0], tri, jnp.zeros((rows, 1), F32), j < t_row)
        acc_ref[...] = _diag_heads(o, n_heads, dec_seq)
        carry_ref[...] = carry

    q = q_ref[0]
    outs, carries = _sb_tiles([q] * g, [_gather_heads(r.at[0], n_heads) for r in kp_refs],
                              [r[0].astype(BF16) for r in vp_refs], tri, carry_ref[...], None, True,
                              expand=(spread_ref[...], own_ref[...]))
    acc = acc_ref[...]
    for o in outs:
        acc = acc + o
    acc_ref[...] = acc
    carry_ref[...] = carries[-1]

    @pl.when(p == pl.num_programs(1) - 1)
    def _():
        o_ref[0] = acc_ref[...].astype(o_ref.dtype)


def sb_sample(page_table, q_bd, k_pool, v_pool, k_new, v_new, n_heads, dec_seq):
    b, rows, w = q_bd.shape
    n_pages = page_table.shape[1]
    g = SB_PAGES_PER_STEP
    page_rows = PAGE_SIZE * n_heads
    cache_row = np.arange(page_rows)
    spread = jnp.asarray(cache_row[None, :] // n_heads == np.arange(PAGE_SIZE)[:, None], BF16)
    own_head = jnp.asarray(cache_row[None, :] % n_heads == (np.arange(rows) // dec_seq)[:, None], BF16)
    pool_specs = [pl.BlockSpec((1, page_rows, HEAD_DIM),
                               lambda bi, p, pt, i=i: (pt[bi, n_pages - 1 - (p * g + i)], 0, 0)) for i in range(g)]
    new_spec = pl.BlockSpec((1, PAGE_SIZE, w), lambda bi, p, pt: (bi, 0, 0))
    const = lambda a: pl.BlockSpec(a.shape, lambda bi, p, pt: (0, 0))
    grid_spec = pltpu.PrefetchScalarGridSpec(
        num_scalar_prefetch=1,
        grid=(b, n_pages // g),
        in_specs=[pl.BlockSpec((1, rows, w), lambda bi, p, pt: (bi, 0, 0)),
                  *pool_specs, *pool_specs, new_spec, new_spec, const(spread), const(own_head)],
        out_specs=pl.BlockSpec((1, rows, HEAD_DIM), lambda bi, p, pt: (bi, 0, 0)),
        scratch_shapes=[pltpu.VMEM((rows, HEAD_DIM), F32), pltpu.VMEM((rows, 1), F32)],
    )
    return pl.pallas_call(
        functools.partial(_sb_sample_kernel, n_heads=n_heads, dec_seq=dec_seq),
        grid_spec=grid_spec,
        out_shape=jax.ShapeDtypeStruct((b, rows, HEAD_DIM), BF16),
        compiler_params=_cparams(("parallel", "arbitrary")),
        name="sb_sample",
    )(page_table, q_bd, *([k_pool] * g), *([v_pool] * g), k_new, v_new, spread, own_head)


def _cmp_sample_kernel(pt_ref, q_ref, *refs, n_pages, dec_seq):
    kp_refs, vp_refs = refs[:n_pages], refs[n_pages:2 * n_pages]
    (w1k_ref, bk_ref, w2k_ref, w1v_ref, bv_ref, w2v_ref, m_ref, oc_ref, sel_ref, xh_ref, acc_ref) = refs[2 * n_pages:]
    past_len = n_pages * PAGE_SIZE
    n_chunk = past_len // CMP_STRIDE
    n_slc = -(-(past_len + dec_seq) // SLC_LEN)

    def compress(page_refs, w1_ref, b_ref, w2_ref):
        for i in range(n_pages):
            _split_heads(page_refs[i].at[0], xh_ref, i * PAGE_SIZE)
        return _compress_heads(xh_ref, w1_ref, b_ref, w2_ref, acc_ref).astype(BF16)

    ck = compress(kp_refs, w1k_ref, bk_ref, w2k_ref)
    cv = compress(vp_refs, w1v_ref, bv_ref, w2v_ref)
    rows = NSA_GROUP * dec_seq
    t_row = _imod(lax.broadcasted_iota(jnp.int32, (rows, 1), 0), dec_seq)
    qpos = past_len + t_row
    cmp_end = lax.broadcasted_iota(jnp.int32, (rows, n_chunk), 1) * CMP_STRIDE + (CMP_LEN - 1)
    r = lax.broadcasted_iota(jnp.int32, (rows, rows), 0)
    c = lax.broadcasted_iota(jnp.int32, (rows, rows), 1)
    same_t = jnp.where(_imod(r, dec_seq) == _imod(c, dec_seq), 1.0, 0.0).astype(BF16)
    for h in range(NSA_KV_HEADS):
        q = q_ref[0, h * rows:(h + 1) * rows, :]
        cs = slice(h * n_chunk, (h + 1) * n_chunk)
        p_c = _masked_softmax(_dot_nt(q, ck[cs]) * SCALE, cmp_end <= qpos)
        oc_ref[0, h * rows:(h + 1) * rows, :] = _dot(p_c.astype(BF16), cv[cs])
        p_sum = _dot_split_left(same_t, p_c)
        imp = _dot_split(p_sum, m_ref[...])
        sel_ref[0, h * rows:(h + 1) * rows, :] = _select_blocks(imp, qpos, n_slc)


def _page_specs(n_pages, page_rows):
    return [pl.BlockSpec((1, page_rows, HEAD_DIM), lambda bi, pt, i=i: (pt[bi, i], 0, 0)) for i in range(n_pages)]


def cmp_sample(page_table, q_rows, k_pool, v_pool, wk, wv, m_mat, dec_seq):
    b, rows, _ = q_rows.shape
    n_pages = page_table.shape[1]
    past_len = n_pages * PAGE_SIZE
    n_chunk = past_len // CMP_STRIDE
    pages = _page_specs(n_pages, PAGE_SIZE * NSA_KV_HEADS)
    const = lambda a: pl.BlockSpec(a.shape, lambda bi, pt: (0,) * a.ndim)
    row_spec = pl.BlockSpec((1, rows, HEAD_DIM), lambda bi, pt: (bi, 0, 0))
    grid_spec = pltpu.PrefetchScalarGridSpec(
        num_scalar_prefetch=1,
        grid=(b,),
        in_specs=[row_spec, *pages, *pages, *[const(a) for a in (*wk, *wv, m_mat)]],
        out_specs=[row_spec, row_spec],
        scratch_shapes=[pltpu.VMEM((NSA_KV_HEADS, past_len, HEAD_DIM), F32),
                        pltpu.VMEM((NSA_KV_HEADS * n_chunk, 2 * CMP_HID), F32)],
    )
    return pl.pallas_call(
        functools.partial(_cmp_sample_kernel, n_pages=n_pages, dec_seq=dec_seq),
        grid_spec=grid_spec,
        out_shape=[jax.ShapeDtypeStruct((b, rows, HEAD_DIM), F32)] * 2,
        compiler_params=_cparams(("parallel",)),
        name="cmp_sample",
    )(page_table, q_rows, *([k_pool] * n_pages), *([v_pool] * n_pages), *wk, *wv, m_mat)


def _nsa_tail_sample_kernel(pt_ref, q_ref, *refs, n_pages, dec_seq):
    kp_refs, vp_refs = refs[:n_pages], refs[n_pages:2 * n_pages]
    (kn_ref, vn_ref, kw_ref, vw_ref, kwn_ref, vwn_ref, oc_ref, sel_ref, g_ref, e_ref, o_ref,
     kall, vall) = refs[2 * n_pages:]
    rows = q_ref.shape[1]
    past_len = n_pages * PAGE_SIZE
    q = q_ref[0]

    for i in range(n_pages):
        kall[i * PAGE_SIZE:(i + 1) * PAGE_SIZE, :] = _gather_heads(kp_refs[i].at[0], NSA_KV_HEADS)
        vall[i * PAGE_SIZE:(i + 1) * PAGE_SIZE, :] = _gather_heads(vp_refs[i].at[0], NSA_KV_HEADS)
    kall[past_len:past_len + PAGE_SIZE, :] = kn_ref[0]
    vall[past_len:past_len + PAGE_SIZE, :] = vn_ref[0]
    n_keys = past_len + PAGE_SIZE
    t_row = _imod(lax.broadcasted_iota(jnp.int32, (rows, n_keys), 0), dec_seq)
    j = lax.broadcasted_iota(jnp.int32, (rows, n_keys), 1)
    chosen = _dot(sel_ref[0].astype(BF16), e_ref[...])
    mask = (chosen > 0.5) & (j - past_len <= t_row)
    p_s = _masked_softmax(_dot_nt(q, kall[...]) * SCALE, mask)
    o_s = _diag_heads(_dot(p_s.astype(BF16), vall[...]), NSA_KV_HEADS, rows // NSA_KV_HEADS)

    k = jnp.concatenate([_gather_heads(kw_ref.at[0], NSA_KV_HEADS), kwn_ref[0]], axis=0)
    v = jnp.concatenate([_gather_heads(vw_ref.at[0], NSA_KV_HEADS), vwn_ref[0]], axis=0)
    n_buf = kw_ref.shape[1] // NSA_KV_HEADS
    n_wkeys = k.shape[0]
    t_row = _imod(lax.broadcasted_iota(jnp.int32, (rows, n_wkeys), 0), dec_seq)
    off = lax.broadcasted_iota(jnp.int32, (rows, n_wkeys), 1) - n_buf
    mask_w = (off <= t_row) & (off > t_row - WINDOW) & (off < dec_seq)
    p_w = _masked_softmax(_dot_nt(q, k) * SCALE, mask_w)
    o_w = _diag_heads(_dot(p_w.astype(BF16), v), NSA_KV_HEADS, rows // NSA_KV_HEADS)

    g = g_ref[0]
    o_ref[0] = (g[:, 0:1] * oc_ref[0] + g[:, 1:2] * o_s + g[:, 2:3] * o_w).astype(o_ref.dtype)


def block_to_key_matrix(n_keys):
    blk = np.arange(LANES)[:, None]
    key = np.arange(n_keys)[None, :]
    return jnp.asarray((blk == key // SLC_LEN).astype(np.float32)).astype(BF16)


def nsa_tail_sample(page_table, q_bd, k_pool, v_pool, k_new, v_new, kw, vw, kw_new, vw_new, o_c, sel, gates, dec_seq):
    b, rows, w = q_bd.shape
    n_pages = page_table.shape[1]
    n_keys = (n_pages + 1) * PAGE_SIZE
    pages = _page_specs(n_pages, PAGE_SIZE * NSA_KV_HEADS)
    e_mat = block_to_key_matrix(n_keys)
    per_b = lambda a: pl.BlockSpec((1, *a.shape[1:]), lambda bi, pt: (bi, 0, 0))
    grid_spec = pltpu.PrefetchScalarGridSpec(
        num_scalar_prefetch=1,
        grid=(b,),
        in_specs=[per_b(q_bd), *pages, *pages,
                  *[per_b(a) for a in (k_new, v_new, kw, vw, kw_new, vw_new, o_c, sel, gates)],
                  pl.BlockSpec(e_mat.shape, lambda bi, pt: (0, 0))],
        out_specs=pl.BlockSpec((1, rows, HEAD_DIM), lambda bi, pt: (bi, 0, 0)),
        scratch_shapes=[pltpu.VMEM((n_keys, w), BF16), pltpu.VMEM((n_keys, w), BF16)],
    )
    return pl.pallas_call(
        functools.partial(_nsa_tail_sample_kernel, n_pages=n_pages, dec_seq=dec_seq),
        grid_spec=grid_spec,
        out_shape=jax.ShapeDtypeStruct((b, rows, HEAD_DIM), BF16),
        compiler_params=_cparams(("parallel",)),
        name="nsa_tail_sample",
    )(page_table, q_bd, *([k_pool] * n_pages), *([v_pool] * n_pages), k_new, v_new, kw, vw, kw_new, vw_new,
      o_c, sel, gates, e_mat)


def prep_weights(w_in, w_branch_a, w_branch_b, w_out, w_ffn_gate, w_ffn_up, w_ffn_down, sb_w, nsa_w, kv_w, d_model):
    n_gate = 3 * (nsa_w // HEAD_DIM)
    widths = (sb_w, sb_w, sb_w, nsa_w, kv_w, kv_w, kv_w, kv_w, kv_w, kv_w, n_gate, d_model, d_model)
    offs = np.concatenate([[0], np.cumsum(widths)])
    parts = [w_in[:, int(offs[i]):int(offs[i + 1])].astype(BF16) for i in range(len(widths))]
    parts[10] = jnp.pad(parts[10], ((0, 0), (0, LANES - n_gate)))
    return dict(w_in=parts, wa=w_branch_a.astype(BF16), wb=w_branch_b.astype(BF16), w_out=w_out.astype(BF16),
                wg=w_ffn_gate.astype(BF16), wu=w_ffn_up.astype(BF16), wd=w_ffn_down.astype(BF16))


def projections(x2d, g_mix, w_parts, tables, table_rows):
    xn = rmsnorm(x2d, g_mix, BF16)
    rope = lambda w, dt: mm(xn, w, dt, epi="rope", extras=tables, table_rows=table_rows)
    plain = lambda w, dt: mm(xn, w, dt)
    q_sb, k_sb, v_sb = plain(w_parts[0], BF16), plain(w_parts[1], F32), plain(w_parts[2], F32)
    q_n = rope(w_parts[3], BF16)
    k_c, v_c = rope(w_parts[4], F32), plain(w_parts[5], F32)
    k_s, v_s = rope(w_parts[6], F32), plain(w_parts[7], F32)
    k_w, v_w = rope(w_parts[8], F32), plain(w_parts[9], F32)
    g_n = mm(xn, w_parts[10], F32, epi="sigmoid")
    g_a, g_b = plain(w_parts[11], F32), plain(w_parts[12], F32)
    return q_sb, k_sb, v_sb, q_n, k_c, v_c, k_s, v_s, k_w, v_w, g_n, g_a, g_b


def merge_and_ffn(x2d, o_sb, o_n, g_a, g_b, wts, g_ffn, g_final):
    mixed = merge(o_sb, o_n, wts["wa"], wts["wb"], g_a, g_b)
    h = mm(mixed, wts["w_out"], F32, epi="residual", extras=(x2d,))
    hn = rmsnorm(h, g_ffn, BF16)
    act = ffn_up(hn, wts["wg"], wts["wu"])
    y = mm(act, wts["wd"], F32, epi="residual", extras=(h,), tm=512, tn=256)
    return rmsnorm(y, g_final, F32)


def prompt_layer(x, wts, cw_k, cw_v, g_mix, g_ffn, g_final):
    b, t, d = x.shape
    x2d = x.reshape(b * t, d)
    tables = rope_tables(jnp.arange(t))
    (q_sb, k_sb, v_sb, q_n, k_c, v_c, k_s, v_s, k_w, v_w, g_n, g_a, g_b) = projections(
        x2d, g_mix, wts["w_in"], tables, t)
    r3 = lambda a: a.reshape(b, t, -1)
    n_sb_heads = q_sb.shape[1] // HEAD_DIM
    o_sb = sb_prompt(r3(q_sb), r3(k_sb), r3(v_sb), n_sb_heads)

    n_chunk = t // CMP_STRIDE
    ck = compress_prompt(r3(k_c), *cw_k).reshape(b, NSA_KV_HEADS, n_chunk, HEAD_DIM)
    cv = compress_prompt(r3(v_c), *cw_v).reshape(b, NSA_KV_HEADS, n_chunk, HEAD_DIM)
    n_gate = 3 * NSA_GROUP
    gates = g_n[:, :NSA_KV_HEADS * n_gate].reshape(b, t, NSA_KV_HEADS, n_gate).transpose(0, 2, 1, 3)
    gates = jnp.pad(gates, ((0, 0), (0, 0), (0, 0), (0, LANES - n_gate)))
    m_mat = cmp_to_slc_matrix(n_chunk, n_chunk - CMP_LEN // CMP_STRIDE + 1, t // SLC_LEN)
    o_n = nsa_prompt(r3(q_n), ck, cv, r3(k_s), r3(v_s), r3(k_w), r3(v_w), gates, m_mat)

    y = merge_and_ffn(x2d, o_sb.reshape(b * t, -1), o_n.reshape(b * t, -1), g_a, g_b, wts, g_ffn, g_final)
    heads = lambda a: a.reshape(1, b, t, -1, HEAD_DIM)
    keep = min(WINDOW, t)
    states = (heads(k_sb), heads(v_sb), heads(k_c), heads(v_c), heads(k_s), heads(v_s),
              heads(k_w)[:, :, t - keep:], heads(v_w)[:, :, t - keep:])
    return y.reshape(b, t, d), states


def _block_diag_rows(q, n_heads):
    b, r = q.shape[:2]
    qt = q.transpose(0, 2, 1, 3)
    eye = jnp.eye(n_heads, dtype=q.dtype)
    return (qt[:, :, :, None, :] * eye[None, :, None, :, None]).reshape(b, n_heads * r, n_heads * HEAD_DIM)


def _pad_new(a, b, t):
    return jnp.pad(a.reshape(b, t, -1).astype(BF16), ((0, 0), (0, PAGE_SIZE - t), (0, 0)))


def sample_layer(x, page_table, caches, wts, cw_k, cw_v, g_mix, g_ffn, g_final):
    c_sb_k, c_sb_v, c_cmp_k, c_cmp_v, c_slc_k, c_slc_v, w_buf_k, w_buf_v = caches
    b, t, d = x.shape
    n_pages = page_table.shape[1]
    past_len = n_pages * PAGE_SIZE
    x2d = x.reshape(b * t, d)
    tables = rope_tables(past_len + jnp.arange(b * t) % t)
    (q_sb, k_sb, v_sb, q_n, k_c, v_c, k_s, v_s, k_w, v_w, g_n, g_a, g_b) = projections(
        x2d, g_mix, wts["w_in"], tables, b * t)
    pool_rows = lambda a: a.reshape(a.shape[0], -1, HEAD_DIM)

    n_sb_heads = q_sb.shape[1] // HEAD_DIM
    q_bd = _block_diag_rows(q_sb.reshape(b, t, n_sb_heads, HEAD_DIM), n_sb_heads)
    o_sb = sb_sample(page_table, q_bd, pool_rows(c_sb_k), pool_rows(c_sb_v), _pad_new(k_sb, b, t),
                     _pad_new(v_sb, b, t), n_sb_heads, t)
    o_sb = o_sb.reshape(b, n_sb_heads, t, HEAD_DIM).transpose(0, 2, 1, 3).reshape(b * t, -1)

    qn5 = q_n.reshape(b, t, NSA_KV_HEADS, NSA_GROUP, HEAD_DIM).transpose(0, 2, 3, 1, 4)
    rows = NSA_KV_HEADS * NSA_GROUP * t
    q_rows = qn5.reshape(b, rows, HEAD_DIM)
    qn_bd = _block_diag_rows(qn5.reshape(b, NSA_KV_HEADS, NSA_GROUP * t, HEAD_DIM).transpose(0, 2, 1, 3), NSA_KV_HEADS)
    n_chunk = past_len // CMP_STRIDE
    n_slc = -(-(past_len + t) // SLC_LEN)
    m_mat = cmp_to_slc_matrix(n_chunk, n_chunk - CMP_LEN // CMP_STRIDE + 1, n_slc)
    o_c, sel = cmp_sample(page_table, q_rows, pool_rows(c_cmp_k), pool_rows(c_cmp_v), cw_k, cw_v, m_mat, t)
    gates = g_n[:, :rows // t * 3].reshape(b, t, NSA_KV_HEADS, NSA_GROUP, 3).transpose(0, 2, 3, 1, 4)
    gates = jnp.pad(gates.reshape(b, rows, 3), ((0, 0), (0, 0), (0, LANES - 3)))
    o_n = nsa_tail_sample(page_table, qn_bd, pool_rows(c_slc_k), pool_rows(c_slc_v), _pad_new(k_s, b, t),
                          _pad_new(v_s, b, t), pool_rows(w_buf_k), pool_rows(w_buf_v), _pad_new(k_w, b, t),
                          _pad_new(v_w, b, t), o_c, sel, gates, t)
    o_n = o_n.reshape(b, NSA_KV_HEADS, NSA_GROUP, t, HEAD_DIM).transpose(0, 3, 1, 2, 4).reshape(b * t, -1)

    y = merge_and_ffn(x2d, o_sb, o_n, g_a, g_b, wts, g_ffn, g_final)
    heads = lambda a: a.reshape(1, b, t, -1, HEAD_DIM)
    wk = jnp.concatenate([w_buf_k[None], heads(k_w)], axis=2)[:, :, t:]
    wv = jnp.concatenate([w_buf_v[None], heads(v_w)], axis=2)[:, :, t:]
    states = (heads(k_sb), heads(v_sb), heads(k_c), heads(v_c), heads(k_s), heads(v_s), wk, wv)
    return y.reshape(b, t, d), states


def kernel(x_prompt, x_sample, cache_sb_k, cache_sb_v, cache_cmp_k, cache_cmp_v, cache_slc_k, cache_slc_v,
           state_win_k, state_win_v, page_table, g_mix, w_in, w_cmp_k1, w_cmp_k2, pe_cmp_k, w_cmp_v1, w_cmp_v2,
           pe_cmp_v, w_branch_a, w_branch_b, w_out, g_ffn, w_ffn_gate, w_ffn_up, w_ffn_down, g_final):
    assert w_in.shape[0] == 1, "single-layer trunk"
    d_model = x_prompt.shape[-1]
    sb_w = cache_sb_k.shape[3] * HEAD_DIM
    kv_w = NSA_KV_HEADS * HEAD_DIM
    nsa_w = NSA_KV_HEADS * NSA_GROUP * HEAD_DIM
    wts = prep_weights(w_in[0], w_branch_a[0], w_branch_b[0], w_out[0], w_ffn_gate[0], w_ffn_up[0],
                       w_ffn_down[0], sb_w, nsa_w, kv_w, d_model)
    cw_k = prep_compress_weights(w_cmp_k1[0], pe_cmp_k[0], w_cmp_k2[0])
    cw_v = prep_compress_weights(w_cmp_v1[0], pe_cmp_v[0], w_cmp_v2[0])
    y_p, st_p = prompt_layer(x_prompt, wts, cw_k, cw_v, g_mix[0], g_ffn[0], g_final)
    caches = (cache_sb_k[0], cache_sb_v[0], cache_cmp_k[0], cache_cmp_v[0], cache_slc_k[0], cache_slc_v[0],
              state_win_k[0], state_win_v[0])
    y_s, st_s = sample_layer(x_sample, page_table, caches, wts, cw_k, cw_v, g_mix[0], g_ffn[0], g_final)
    return (y_p, y_s, *st_p, *st_s)
```

```python
import functools

import jax
import jax.numpy as jnp
import numpy as np
from jax import lax
from jax.experimental import pallas as pl
from jax.experimental.pallas import tpu as pltpu

F32 = jnp.float32
BF16 = jnp.bfloat16

HEAD_DIM = 128
LANES = 128
NSA_KV_HEADS = 4
NSA_GROUP = 4
CMP_LEN = 32
CMP_STRIDE = 16
CMP_HID = 2 * HEAD_DIM
SLC_LEN = 64
N_SEL = 16
WINDOW = 512
ROPE_THETA = 500000.0
ROPE_DIM = HEAD_DIM // 4
ROPE_HALF = ROPE_DIM // 2
RMS_EPS = 1e-6
NEG = -1e30
PAGE_SIZE = 128
SCALE = HEAD_DIM ** -0.5
LOG2E = 1.4426950408889634
VMEM_LIMIT = 56 * 1024 * 1024


def _cparams(sem):
    return pltpu.CompilerParams(dimension_semantics=sem, vmem_limit_bytes=VMEM_LIMIT)


def _dot(a, b):
    return jnp.dot(a, b, preferred_element_type=F32)


def _dot_nt(a, b):
    return lax.dot_general(a, b, (((1,), (1,)), ((), ())), preferred_element_type=F32)


def _dot_split(x, m_bf16):
    hi = x.astype(BF16)
    lo = (x - hi.astype(F32)).astype(BF16)
    return _dot(hi, m_bf16) + _dot(lo, m_bf16)


def _dot_split_left(m_bf16, x):
    hi = x.astype(BF16)
    lo = (x - hi.astype(F32)).astype(BF16)
    return _dot(m_bf16, hi) + _dot(m_bf16, lo)


def _idiv(x, d):
    return x >> (d.bit_length() - 1) if d & (d - 1) == 0 else x // d


def _imod(x, d):
    return x & (d - 1) if d & (d - 1) == 0 else x % d


def _softplus(z):
    return jnp.maximum(z, 0.0) + jnp.log(1.0 + jnp.exp(-jnp.abs(z)))


def _rmsnorm_kernel(x_ref, g_ref, o_ref):
    x = x_ref[...]
    y = x * lax.rsqrt(jnp.mean(x * x, axis=-1, keepdims=True) + RMS_EPS)
    o_ref[...] = (y * g_ref[...]).astype(o_ref.dtype)


def rmsnorm(x, g, out_dtype, tm=512):
    m, d = x.shape
    return pl.pallas_call(
        _rmsnorm_kernel,
        grid=(m // tm,),
        in_specs=[pl.BlockSpec((tm, d), lambda i: (i, 0)),
                  pl.BlockSpec((1, d), lambda i: (0, 0))],
        out_specs=pl.BlockSpec((tm, d), lambda i: (i, 0)),
        out_shape=jax.ShapeDtypeStruct((m, d), out_dtype),
        compiler_params=_cparams(("parallel",)),
        name="rmsnorm",
    )(x, g.reshape(1, d))


def _rope_tile(acc, c, s1, s2):
    return (acc * c + pltpu.roll(acc, LANES - ROPE_HALF, axis=1) * s1
            + pltpu.roll(acc, ROPE_HALF, axis=1) * s2)


def _mm_kernel(*refs, epi):
    if epi == "rope":
        x_ref, w_ref, c_ref, s1_ref, s2_ref, o_ref = refs
    elif epi == "residual":
        x_ref, w_ref, r_ref, o_ref = refs
    else:
        x_ref, w_ref, o_ref = refs
    acc = _dot(x_ref[...], w_ref[...])
    if epi == "rope":
        c, s1, s2 = c_ref[...], s1_ref[...], s2_ref[...]
        for hh in range(acc.shape[1] // LANES):
            sl = slice(hh * LANES, (hh + 1) * LANES)
            o_ref[:, sl] = _rope_tile(acc[:, sl], c, s1, s2).astype(o_ref.dtype)
    elif epi == "sigmoid":
        o_ref[...] = jax.nn.sigmoid(acc).astype(o_ref.dtype)
    elif epi == "residual":
        o_ref[...] = (r_ref[...] + acc).astype(o_ref.dtype)
    else:
        o_ref[...] = acc.astype(o_ref.dtype)


def mm(x, w, out_dtype, epi="plain", extras=(), tm=1024, tn=512, table_rows=None):
    m, k = x.shape
    n = w.shape[1]
    tm, tn = min(tm, m), min(tn, n)
    in_specs = [pl.BlockSpec((tm, k), lambda i, j: (i, 0)),
                pl.BlockSpec((k, tn), lambda i, j: (0, j))]
    if epi == "rope":
        nblk = table_rows // tm
        in_specs += [pl.BlockSpec((tm, LANES), lambda i, j: (i % nblk, 0))] * 3
    elif epi == "residual":
        in_specs += [pl.BlockSpec((tm, tn), lambda i, j: (i, j))]
    return pl.pallas_call(
        functools.partial(_mm_kernel, epi=epi),
        grid=(m // tm, n // tn),
        in_specs=in_specs,
        out_specs=pl.BlockSpec((tm, tn), lambda i, j: (i, j)),
        out_shape=jax.ShapeDtypeStruct((m, n), out_dtype),
        compiler_params=_cparams(("parallel", "arbitrary")),
        name="mm_" + epi,
    )(x, w, *extras)


def _merge_kernel(a_ref, b_ref, wa_ref, wb_ref, ga_ref, gb_ref, o_ref):
    ya = _dot(a_ref[...], wa_ref[...])
    yb = _dot(b_ref[...], wb_ref[...])
    o_ref[...] = (jax.nn.sigmoid(ga_ref[...]) * ya + jax.nn.sigmoid(gb_ref[...]) * yb).astype(o_ref.dtype)


def merge(o_a, o_b, wa, wb, g_a, g_b, tm=1024, tn=512):
    m, k = o_a.shape
    n = wa.shape[1]
    tm = min(tm, m)
    xs = pl.BlockSpec((tm, k), lambda i, j: (i, 0))
    ws = pl.BlockSpec((k, tn), lambda i, j: (0, j))
    gs = pl.BlockSpec((tm, tn), lambda i, j: (i, j))
    return pl.pallas_call(
        _merge_kernel,
        grid=(m // tm, n // tn),
        in_specs=[xs, xs, ws, ws, gs, gs],
        out_specs=gs,
        out_shape=jax.ShapeDtypeStruct((m, n), BF16),
        compiler_params=_cparams(("parallel", "arbitrary")),
        name="merge",
    )(o_a, o_b, wa, wb, g_a, g_b)


def _ffn_up_kernel(x_ref, wg_ref, wu_ref, o_ref):
    x = x_ref[...]
    a = _dot(x, wg_ref[...])
    b = _dot(x, wu_ref[...])
    o_ref[...] = (a * jax.nn.sigmoid(a) * b).astype(o_ref.dtype)


def ffn_up(x, wg, wu, tm=1024, tn=256):
    m, k = x.shape
    n = wg.shape[1]
    tm = min(tm, m)
    ws = pl.BlockSpec((k, tn), lambda i, j: (0, j))
    return pl.pallas_call(
        _ffn_up_kernel,
        grid=(m // tm, n // tn),
        in_specs=[pl.BlockSpec((tm, k), lambda i, j: (i, 0)), ws, ws],
        out_specs=pl.BlockSpec((tm, tn), lambda i, j: (i, j)),
        out_shape=jax.ShapeDtypeStruct((m, n), BF16),
        compiler_params=_cparams(("parallel", "arbitrary")),
        name="ffn_up",
    )(x, wg, wu)


def rope_tables(pos):
    inv = jnp.float32(ROPE_THETA) ** (-(jnp.arange(ROPE_HALF, dtype=F32) * 2.0 / ROPE_DIM))
    ang = pos.astype(F32)[:, None] * inv[None, :]
    cos, sin = jnp.cos(ang), jnp.sin(ang)
    n = pos.shape[0]
    rest = HEAD_DIM - ROPE_DIM
    c = jnp.concatenate([cos, cos, jnp.ones((n, rest), F32)], axis=1)
    s1 = jnp.concatenate([-sin, jnp.zeros((n, HEAD_DIM - ROPE_HALF), F32)], axis=1)
    s2 = jnp.concatenate([jnp.zeros((n, ROPE_HALF), F32), sin, jnp.zeros((n, rest), F32)], axis=1)
    return c, s1, s2


SB_BLK = 256
SB_HEADS_PER_STEP = 4


def _strict_lower(n):
    r = lax.broadcasted_iota(jnp.int32, (n, n), 0)
    c = lax.broadcasted_iota(jnp.int32, (n, n), 1)
    return jnp.where(r > c, 1.0, 0.0).astype(BF16)


def _sb_tiles(qs, ks, vs, tri, carry_in, mask, chained, expand=None):
    zs = [_dot_nt(q, k) * SCALE for q, k in zip(qs, ks)]
    pre, lgs, his, los = [], [], [], []
    for z in zs:
        sp = _softplus(z)
        lg = -sp if mask is None else jnp.where(mask, -sp, 0.0)
        hi = lg.astype(BF16)
        pre.append(z - sp)
        lgs.append(lg)
        his.append(hi)
        los.append((lg - hi.astype(F32)).astype(BF16))
    cs_hi = [_dot(hi, tri) for hi in his]
    cs_lo = [_dot(lo, tri) for lo in los]
    carries, probs = [], []
    carry = carry_in
    for i, lg in enumerate(lgs):
        c0 = carry if chained else carry_in[i]
        a = jnp.exp(pre[i] + (cs_hi[i] + cs_lo[i]) + c0)
        if mask is not None:
            a = jnp.where(mask, a, 0.0)
        probs.append(a.astype(BF16))
        carry = c0 + jnp.sum(lg, axis=1, keepdims=True)
        carries.append(carry)
    if expand is not None:
        spread, own_head = expand
        probs = [_dot(a, spread).astype(BF16) * own_head for a in probs]
    return [_dot(a, v) for a, v in zip(probs, vs)], carries


def _sb_tile(q, k, v, tri, carry, mask):
    outs, carries = _sb_tiles([q], [k], [v], tri, [carry], mask, False)
    return outs[0], carries[0]


def _sb_prompt_kernel(q_ref, k_ref, v_ref, o_ref, kb_ref, vb_ref):
    t = q_ref.shape[1]
    kb_ref[...] = k_ref[0].astype(BF16)
    vb_ref[...] = v_ref[0].astype(BF16)
    tri = _strict_lower(SB_BLK)
    r = lax.broadcasted_iota(jnp.int32, (SB_BLK, SB_BLK), 0)
    c = lax.broadcasted_iota(jnp.int32, (SB_BLK, SB_BLK), 1)
    diag_mask = c < r
    heads = [slice(h * HEAD_DIM, (h + 1) * HEAD_DIM) for h in range(SB_HEADS_PER_STEP)]

    def q_block(qi, _):
        q0 = pl.multiple_of(qi * SB_BLK, SB_BLK)
        qs = [q_ref[0, pl.ds(q0, SB_BLK), hs] for hs in heads]

        def tiles(k0, carries, mask):
            return _sb_tiles(qs, [kb_ref[pl.ds(k0, SB_BLK), hs] for hs in heads],
                             [vb_ref[pl.ds(k0, SB_BLK), hs] for hs in heads], tri, carries, mask, False)

        accs, carries = tiles(q0, [jnp.zeros((SB_BLK, 1), F32)] * len(heads), diag_mask)

        def k_block(step, st):
            accs, carries = st
            outs, carries = tiles(pl.multiple_of((qi - 1 - step) * SB_BLK, SB_BLK), carries, None)
            return tuple(a + o for a, o in zip(accs, outs)), tuple(carries)

        accs, _ = lax.fori_loop(0, qi, k_block, (tuple(accs), tuple(carries)))
        for hs, acc in zip(heads, accs):
            o_ref[0, pl.ds(q0, SB_BLK), hs] = acc.astype(o_ref.dtype)
        return 0

    lax.fori_loop(0, t // SB_BLK, q_block, 0)


def sb_prompt(q, k, v, n_heads):
    b, t, _ = q.shape
    w = SB_HEADS_PER_STEP * HEAD_DIM
    spec = pl.BlockSpec((1, t, w), lambda bi, h: (bi, 0, h))
    return pl.pallas_call(
        _sb_prompt_kernel,
        grid=(b, n_heads // SB_HEADS_PER_STEP),
        in_specs=[spec, spec, spec],
        out_specs=spec,
        out_shape=jax.ShapeDtypeStruct(q.shape, BF16),
        scratch_shapes=[pltpu.VMEM((t, w), BF16), pltpu.VMEM((t, w), BF16)],
        compiler_params=_cparams(("parallel", "parallel")),
        name="sb_prompt",
    )(q, k, v)


CMP_PITCH = 20


def _split_heads(x_ref, xh_ref, chunk0):
    n = x_ref.shape[0] // NSA_KV_HEADS
    for c in range(n // CMP_STRIDE):
        for h in range(NSA_KV_HEADS):
            r0 = (chunk0 + c) * CMP_PITCH
            xh_ref[h, r0:r0 + CMP_STRIDE, :] = x_ref[
                pl.ds(c * CMP_STRIDE * NSA_KV_HEADS + h, CMP_STRIDE, stride=NSA_KV_HEADS), :]


def _compress_heads(xh_refs, weights):
    n_chunk = xh_refs[0].shape[1] // CMP_PITCH
    rows = NSA_KV_HEADS * n_chunk
    lhss = [jnp.concatenate(
        [jnp.concatenate([xh_ref[h, pl.ds(s, n_chunk, stride=CMP_PITCH), :].astype(BF16)
                          for s in range(CMP_STRIDE)], axis=1)
         for h in range(NSA_KV_HEADS)], axis=0) for xh_ref in xh_refs]
    accs = [_dot(lhs, w1_ref[...]) for lhs, (w1_ref, _, _) in zip(lhss, weights)]
    hids = []
    for acc, (_, bias_ref, _) in zip(accs, weights):
        hid = acc[:, :CMP_HID] + pltpu.roll(acc[:, CMP_HID:], rows - 1, axis=0) + bias_ref[0:1, :]
        hids.append((hid * jax.nn.sigmoid(hid)).astype(BF16))
    outs = [_dot(hid, w2_ref[...]) for hid, (_, _, w2_ref) in zip(hids, weights)]
    chunk = _imod(lax.broadcasted_iota(jnp.int32, (rows, HEAD_DIM), 0), n_chunk)
    return [jnp.where(chunk < n_chunk - 1, out, 0.0) for out in outs]


def _compress_kernel(x_ref, w1_ref, bias_ref, w2_ref, o_ref, xh_ref):
    _split_heads(x_ref.at[0], xh_ref, 0)
    o_ref[0] = _compress_heads([xh_ref], [(w1_ref, bias_ref, w2_ref)])[0].astype(o_ref.dtype)


def compress_prompt(x, w1r, bias, w2):
    b, t, _ = x.shape
    rows = NSA_KV_HEADS * (t // CMP_STRIDE)
    x = x.reshape(b, t * NSA_KV_HEADS, HEAD_DIM)
    return pl.pallas_call(
        _compress_kernel,
        grid=(b,),
        in_specs=[pl.BlockSpec((1, t * NSA_KV_HEADS, HEAD_DIM), lambda i: (i, 0, 0)),
                  pl.BlockSpec(w1r.shape, lambda i: (0, 0)),
                  pl.BlockSpec(bias.shape, lambda i: (0, 0)),
                  pl.BlockSpec(w2.shape, lambda i: (0, 0))],
        out_specs=pl.BlockSpec((1, rows, HEAD_DIM), lambda i: (i, 0, 0)),
        out_shape=jax.ShapeDtypeStruct((b, rows, HEAD_DIM), BF16),
        scratch_shapes=[pltpu.VMEM((NSA_KV_HEADS, t // CMP_STRIDE * CMP_PITCH, HEAD_DIM), F32)],
        compiler_params=_cparams(("parallel",)),
        name="compress_prompt",
    )(x, w1r, bias, w2)


def prep_compress_weights(w1, pe, w2):
    r = CMP_LEN // CMP_STRIDE
    w1r = w1.reshape(r, CMP_STRIDE, HEAD_DIM, CMP_HID).transpose(1, 2, 0, 3)
    w1r = w1r.reshape(CMP_STRIDE * HEAD_DIM, r * CMP_HID).astype(BF16)
    pe_rows = jnp.zeros((16, CMP_LEN * HEAD_DIM), F32).at[0].set(pe.reshape(-1)).astype(BF16)
    bias = mm(pe_rows, w1.reshape(CMP_LEN * HEAD_DIM, CMP_HID).astype(BF16), F32, tn=CMP_HID)
    return w1r, bias, w2.astype(BF16)


def cmp_to_slc_matrix(n_cmp_pad, n_cmp, n_slc):
    i = np.arange(n_cmp_pad)[:, None]
    j = np.arange(LANES)[None, :]
    lo = np.maximum(i * CMP_STRIDE, j * SLC_LEN)
    hi = np.minimum(i * CMP_STRIDE + CMP_LEN, j * SLC_LEN + SLC_LEN)
    m = np.clip(hi - lo, 0, None) / CMP_LEN
    m = np.where((i < n_cmp) & (j < n_slc), m, 0.0)
    return jnp.asarray(m.astype(np.float32)).astype(BF16)


def _masked_softmax(s, mask):
    s = jnp.where(mask, s, NEG)
    m = jnp.max(s, axis=1, keepdims=True)
    e = jnp.where(mask, jnp.exp(s - m), 0.0)
    den = jnp.sum(e, axis=1, keepdims=True)
    return e / jnp.where(den > 0.0, den, 1.0)


def _select_blocks(imp, qpos, n_slc):
    blk = lax.broadcasted_iota(jnp.int32, imp.shape, 1)
    cur = _idiv(qpos, SLC_LEN)
    valid = (blk * SLC_LEN <= qpos) & (blk < n_slc)
    forced = (blk == 0) | (blk == cur) | (blk == cur - 1)
    score = jnp.where(valid, jnp.where(forced, jnp.inf, imp), -jnp.inf)
    rank = jnp.zeros(imp.shape, F32)
    for i in range(n_slc):
        col = score[:, i:i + 1]
        ge = jnp.where(col >= score, 1.0, 0.0)
        gt = jnp.where(col > score, 1.0, 0.0)
        rank = rank + jnp.where(blk > i, ge, gt)
    return jnp.where(valid & (rank < float(min(N_SEL, n_slc))), 1.0, 0.0)


def _select_blocks_t(imp_t, qpos_row, n_slc):
    blk = lax.broadcasted_iota(jnp.int32, imp_t.shape, 0)
    cur = _idiv(qpos_row, SLC_LEN)
    valid = blk * SLC_LEN <= qpos_row
    forced = (blk == 0) | (blk == cur) | (blk == cur - 1)
    score = jnp.where(valid, jnp.where(forced, jnp.inf, imp_t), -jnp.inf)
    rank = jnp.zeros(imp_t.shape, F32)
    for i in range(n_slc):
        row = score[i:i + 1, :]
        ge = jnp.where(row >= score, 1.0, 0.0)
        gt = jnp.where(row > score, 1.0, 0.0)
        rank = rank + jnp.where(blk > i, ge, gt)
    return jnp.where(valid & (rank < float(min(N_SEL, n_slc))), 1.0, 0.0)


NSA_QB = 128
NSA_KB = 512


def _nsa_prompt_kernel(q_ref, ck_ref, cv_ref, ks_ref, vs_ref, kw_ref, vw_ref, g_ref, mt_ref, o_ref,
                       ksb, vsb, kwb, vwb):
    t = ks_ref.shape[1]
    qb = pl.program_id(2)
    n_slc = t // SLC_LEN

    @pl.when(qb == 0)
    def _():
        ksb[...] = ks_ref[0].astype(BF16)
        vsb[...] = vs_ref[0].astype(BF16)
        kwb[...] = kw_ref[0].astype(BF16)
        vwb[...] = vw_ref[0].astype(BF16)

    rows = NSA_GROUP * NSA_QB
    q0 = qb * NSA_QB
    q = jnp.concatenate([q_ref[0, :, g * HEAD_DIM:(g + 1) * HEAD_DIM] for g in range(NSA_GROUP)], axis=0)
    qpos1 = q0 + lax.broadcasted_iota(jnp.int32, (NSA_QB, 1), 0)
    qpos = jnp.concatenate([qpos1] * NSA_GROUP, axis=0)

    tile_rows = lambda a: jnp.concatenate([a] * NSA_GROUP, axis=0)

    n_cmp_pad = ck_ref.shape[2]
    band = WINDOW + NSA_QB
    w0 = pl.multiple_of(jnp.maximum(q0 - WINDOW, 0), NSA_QB)
    s_c = _dot_nt(q, ck_ref[0, 0]) * (SCALE * LOG2E)
    s_w = _dot_nt(q, kwb[pl.ds(w0, band), :]) * (SCALE * LOG2E)

    cmp_end = lax.broadcasted_iota(jnp.int32, (rows, n_cmp_pad), 1) * CMP_STRIDE + (CMP_LEN - 1)
    mask_c = cmp_end <= qpos
    s_c = jnp.where(mask_c, s_c, NEG)
    e_c = jnp.where(mask_c, jnp.exp2(s_c - jnp.max(s_c, axis=1, keepdims=True)), 0.0)
    den_c = jnp.sum(e_c, axis=1, keepdims=True)
    p_c = e_c * (1.0 / jnp.where(den_c > 0.0, den_c, 1.0))
    o_c = _dot(p_c.astype(BF16), cv_ref[0, 0])

    dist = qpos1 - (w0 + lax.broadcasted_iota(jnp.int32, (NSA_QB, band), 1))
    s_w = s_w + tile_rows(jnp.where((dist >= 0) & (dist < WINDOW), 0.0, NEG))
    e_w = jnp.exp2(s_w - jnp.max(s_w, axis=1, keepdims=True))
    p_w = e_w * (1.0 / jnp.sum(e_w, axis=1, keepdims=True))
    o_w = _dot(p_w.astype(BF16), vwb[pl.ds(w0, band), :])

    p_sum = p_c[0:NSA_QB]
    for g in range(1, NSA_GROUP):
        p_sum = p_sum + p_c[g * NSA_QB:(g + 1) * NSA_QB]
    p_hi = p_sum.astype(BF16)
    p_lo = (p_sum - p_hi.astype(F32)).astype(BF16)
    imp_t = _dot_nt(mt_ref[...], p_hi) + _dot_nt(mt_ref[...], p_lo)
    qpos_row = q0 + lax.broadcasted_iota(jnp.int32, (1, NSA_QB), 1)
    sel_t = _select_blocks_t(imp_t, qpos_row, n_slc).astype(BF16)
    eye = jnp.where(lax.broadcasted_iota(jnp.int32, (NSA_QB, NSA_QB), 0)
                    == lax.broadcasted_iota(jnp.int32, (NSA_QB, NSA_QB), 1), 1.0, 0.0).astype(BF16)
    sel = _dot_nt(eye, sel_t).astype(BF16)

    def slc_chunk(kc, st):
        m_i, l_i, acc = st
        k0 = pl.multiple_of(kc * NSA_KB, NSA_KB)
        blk_of_key = _idiv(k0 + lax.broadcasted_iota(jnp.int32, (n_slc, NSA_KB), 1), SLC_LEN)
        expand = jnp.where(lax.broadcasted_iota(jnp.int32, (n_slc, NSA_KB), 0) == blk_of_key, 1.0, 0.0)
        chosen = _dot(sel, expand.astype(BF16))
        kpos = k0 + lax.broadcasted_iota(jnp.int32, (NSA_QB, NSA_KB), 1)
        bias = tile_rows(jnp.where((chosen > 0.5) & (kpos <= qpos1), 0.0, NEG))
        s = _dot_nt(q, ksb[pl.ds(k0, NSA_KB), :]) * (SCALE * LOG2E) + bias
        m_new = jnp.maximum(m_i, jnp.max(s, axis=1, keepdims=True))
        alpha = jnp.exp2(m_i - m_new)
        e = jnp.exp2(s - m_new)
        l_new = alpha * l_i + jnp.sum(e, axis=1, keepdims=True)
        acc = alpha * acc + _dot(e.astype(BF16), vsb[pl.ds(k0, NSA_KB), :])
        return m_new, l_new, acc

    n_chunks = (q0 + NSA_QB + NSA_KB - 1) // NSA_KB
    m_i, l_i, acc = lax.fori_loop(
        0, n_chunks, slc_chunk,
        (jnp.full((rows, 1), NEG, F32), jnp.zeros((rows, 1), F32), jnp.zeros((rows, HEAD_DIM), F32)))
    o_s = acc * (1.0 / l_i)

    gates = g_ref[0, 0]
    for g in range(NSA_GROUP):
        rs = slice(g * NSA_QB, (g + 1) * NSA_QB)
        o = (gates[:, 3 * g:3 * g + 1] * o_c[rs] + gates[:, 3 * g + 1:3 * g + 2] * o_s[rs]
             + gates[:, 3 * g + 2:3 * g + 3] * o_w[rs])
        o_ref[0, :, g * HEAD_DIM:(g + 1) * HEAD_DIM] = o.astype(o_ref.dtype)


def nsa_prompt(q, ck, cv, ks, vs, kw, vw, gates, m_mat):
    b, t, _ = q.shape
    m_mat = m_mat[:, :t // SLC_LEN].T
    gw = NSA_GROUP * HEAD_DIM
    qspec = pl.BlockSpec((1, NSA_QB, gw), lambda bi, h, qb: (bi, qb, h))
    cspec = pl.BlockSpec((1, 1, ck.shape[2], HEAD_DIM), lambda bi, h, qb: (bi, h, 0, 0))
    kspec = pl.BlockSpec((1, t, HEAD_DIM), lambda bi, h, qb: (bi, 0, h))
    return pl.pallas_call(
        _nsa_prompt_kernel,
        grid=(b, NSA_KV_HEADS, t // NSA_QB),
        in_specs=[qspec, cspec, cspec, kspec, kspec, kspec, kspec,
                  pl.BlockSpec((1, 1, NSA_QB, LANES), lambda bi, h, qb: (bi, h, qb, 0)),
                  pl.BlockSpec(m_mat.shape, lambda bi, h, qb: (0, 0))],
        out_specs=qspec,
        out_shape=jax.ShapeDtypeStruct(q.shape, BF16),
        scratch_shapes=[pltpu.VMEM((t, HEAD_DIM), BF16)] * 4,
        compiler_params=_cparams(("parallel", "parallel", "arbitrary")),
        name="nsa_prompt",
    )(q, ck, cv, ks, vs, kw, vw, gates, m_mat)


def _diag_heads(acc, n_heads, rows_per_head):
    return jnp.concatenate(
        [acc[h * rows_per_head:(h + 1) * rows_per_head, h * HEAD_DIM:(h + 1) * HEAD_DIM]
         for h in range(n_heads)], axis=0)


def _gather_heads(x_ref, n_heads):
    n = x_ref.shape[0] // n_heads
    return jnp.concatenate([x_ref[pl.ds(h, n, stride=n_heads), :].astype(BF16) for h in range(n_heads)], axis=1)


SB_PAGES_PER_STEP = 8


def _sb_sample_kernel(pt_ref, q_ref, *refs, n_heads, dec_seq):
    g = SB_PAGES_PER_STEP
    kp_refs, vp_refs = refs[:g], refs[g:2 * g]
    kn_ref, vn_ref, spread_ref, own_ref, o_ref, acc_ref, carry_ref = refs[2 * g:]
    p = pl.program_id(1)
    rows = q_ref.shape[1]
    tri = _strict_lower(PAGE_SIZE)

    @pl.when(p == 0)
    def _():
        t_row = _imod(lax.broadcasted_iota(jnp.int32, (rows, PAGE_SIZE), 0), dec_seq)
        j = lax.broadcasted_iota(jnp.int32, (rows, PAGE_SIZE), 1)
        o, carry = _sb_tile(q_ref[0], kn_ref[0], vn_ref[0], tri, jnp.zeros((rows, 1), F32), j < t_row)
        acc_ref[...] = _diag_heads(o, n_heads, dec_seq)
        carry_ref[...] = carry

    q = q_ref[0]
    outs, carries = _sb_tiles([q] * g, [_gather_heads(r.at[0], n_heads) for r in kp_refs],
                              [r[0].astype(BF16) for r in vp_refs], tri, carry_ref[...], None, True,
                              expand=(spread_ref[...], own_ref[...]))
    acc = acc_ref[...]
    for o in outs:
        acc = acc + o
    acc_ref[...] = acc
    carry_ref[...] = carries[-1]

    @pl.when(p == pl.num_programs(1) - 1)
    def _():
        o_ref[0] = acc_ref[...].astype(o_ref.dtype)


def sb_sample(page_table, q_bd, k_pool, v_pool, k_new, v_new, n_heads, dec_seq):
    b, rows, w = q_bd.shape
    n_pages = page_table.shape[1]
    g = SB_PAGES_PER_STEP
    page_rows = PAGE_SIZE * n_heads
    cache_row = np.arange(page_rows)
    spread = jnp.asarray(cache_row[None, :] // n_heads == np.arange(PAGE_SIZE)[:, None], BF16)
    own_head = jnp.asarray(cache_row[None, :] % n_heads == (np.arange(rows) // dec_seq)[:, None], BF16)
    pool_specs = [pl.BlockSpec((1, page_rows, HEAD_DIM),
                               lambda bi, p, pt, i=i: (pt[bi, n_pages - 1 - (p * g + i)], 0, 0)) for i in range(g)]
    new_spec = pl.BlockSpec((1, PAGE_SIZE, w), lambda bi, p, pt: (bi, 0, 0))
    const = lambda a: pl.BlockSpec(a.shape, lambda bi, p, pt: (0, 0))
    grid_spec = pltpu.PrefetchScalarGridSpec(
        num_scalar_prefetch=1,
        grid=(b, n_pages // g),
        in_specs=[pl.BlockSpec((1, rows, w), lambda bi, p, pt: (bi, 0, 0)),
                  *pool_specs, *pool_specs, new_spec, new_spec, const(spread), const(own_head)],
        out_specs=pl.BlockSpec((1, rows, HEAD_DIM), lambda bi, p, pt: (bi, 0, 0)),
        scratch_shapes=[pltpu.VMEM((rows, HEAD_DIM), F32), pltpu.VMEM((rows, 1), F32)],
    )
    return pl.pallas_call(
        functools.partial(_sb_sample_kernel, n_heads=n_heads, dec_seq=dec_seq),
        grid_spec=grid_spec,
        out_shape=jax.ShapeDtypeStruct((b, rows, HEAD_DIM), BF16),
        compiler_params=_cparams(("parallel", "arbitrary")),
        name="sb_sample",
    )(page_table, q_bd, *([k_pool] * g), *([v_pool] * g), k_new, v_new, spread, own_head)


def _cmp_sample_kernel(pt_ref, q_ref, *refs, n_pages, dec_seq):
    kp_refs, vp_refs = refs[:n_pages], refs[n_pages:2 * n_pages]
    (w1k_ref, bk_ref, w2k_ref, w1v_ref, bv_ref, w2v_ref, m_ref, oc_ref, sel_ref, xk_ref, xv_ref) = refs[2 * n_pages:]
    past_len = n_pages * PAGE_SIZE
    n_chunk = past_len // CMP_STRIDE
    n_slc = -(-(past_len + dec_seq) // SLC_LEN)

    for i in range(n_pages):
        _split_heads(kp_refs[i].at[0], xk_ref, i * (PAGE_SIZE // CMP_STRIDE))
        _split_heads(vp_refs[i].at[0], xv_ref, i * (PAGE_SIZE // CMP_STRIDE))
    ck, cv = [c.astype(BF16) for c in _compress_heads(
        [xk_ref, xv_ref], [(w1k_ref, bk_ref, w2k_ref), (w1v_ref, bv_ref, w2v_ref)])]
    rows = NSA_GROUP * dec_seq
    t_row = _imod(lax.broadcasted_iota(jnp.int32, (rows, 1), 0), dec_seq)
    qpos = past_len + t_row
    cmp_end = lax.broadcasted_iota(jnp.int32, (rows, n_chunk), 1) * CMP_STRIDE + (CMP_LEN - 1)
    r = lax.broadcasted_iota(jnp.int32, (rows, rows), 0)
    c = lax.broadcasted_iota(jnp.int32, (rows, rows), 1)
    same_t = jnp.where(_imod(r, dec_seq) == _imod(c, dec_seq), 1.0, 0.0).astype(BF16)
    for h in range(NSA_KV_HEADS):
        q = q_ref[0, h * rows:(h + 1) * rows, :]
        cs = slice(h * n_chunk, (h + 1) * n_chunk)
        p_c = _masked_softmax(_dot_nt(q, ck[cs]) * SCALE, cmp_end <= qpos)
        oc_ref[0, h * rows:(h + 1) * rows, :] = _dot(p_c.astype(BF16), cv[cs])
        p_sum = _dot_split_left(same_t, p_c)
        imp = _dot_split(p_sum, m_ref[...])
        sel_ref[0, h * rows:(h + 1) * rows, :] = _select_blocks(imp, qpos, n_slc)


def _page_specs(n_pages, page_rows):
    return [pl.BlockSpec((1, page_rows, HEAD_DIM), lambda bi, pt, i=i: (pt[bi, i], 0, 0)) for i in range(n_pages)]


def cmp_sample(page_table, q_rows, k_pool, v_pool, wk, wv, m_mat, dec_seq):
    b, rows, _ = q_rows.shape
    n_pages = page_table.shape[1]
    past_len = n_pages * PAGE_SIZE
    n_chunk = past_len // CMP_STRIDE
    pages = _page_specs(n_pages, PAGE_SIZE * NSA_KV_HEADS)
    const = lambda a: pl.BlockSpec(a.shape, lambda bi, pt: (0,) * a.ndim)
    row_spec = pl.BlockSpec((1, rows, HEAD_DIM), lambda bi, pt: (bi, 0, 0))
    grid_spec = pltpu.PrefetchScalarGridSpec(
        num_scalar_prefetch=1,
        grid=(b,),
        in_specs=[row_spec, *pages, *pages, *[const(a) for a in (*wk, *wv, m_mat)]],
        out_specs=[row_spec, row_spec],
        scratch_shapes=[pltpu.VMEM((NSA_KV_HEADS, n_chunk * CMP_PITCH, HEAD_DIM), F32)] * 2,
    )
    return pl.pallas_call(
        functools.partial(_cmp_sample_kernel, n_pages=n_pages, dec_seq=dec_seq),
        grid_spec=grid_spec,
        out_shape=[jax.ShapeDtypeStruct((b, rows, HEAD_DIM), F32)] * 2,
        compiler_params=_cparams(("parallel",)),
        name="cmp_sample",
    )(page_table, q_rows, *([k_pool] * n_pages), *([v_pool] * n_pages), *wk, *wv, m_mat)


def _nsa_tail_sample_kernel(pt_ref, q_ref, *refs, n_pages, dec_seq):
    kp_refs, vp_refs = refs[:n_pages], refs[n_pages:2 * n_pages]
    (kn_ref, vn_ref, kw_ref, vw_ref, kwn_ref, vwn_ref, oc_ref, sel_ref, g_ref, e_ref, o_ref,
     kall, vall) = refs[2 * n_pages:]
    rows = q_ref.shape[1]
    past_len = n_pages * PAGE_SIZE
    q = q_ref[0]

    for i in range(n_pages):
        kall[i * PAGE_SIZE:(i + 1) * PAGE_SIZE, :] = _gather_heads(kp_refs[i].at[0], NSA_KV_HEADS)
        vall[i * PAGE_SIZE:(i + 1) * PAGE_SIZE, :] = _gather_heads(vp_refs[i].at[0], NSA_KV_HEADS)
    kall[past_len:past_len + PAGE_SIZE, :] = kn_ref[0]
    vall[past_len:past_len + PAGE_SIZE, :] = vn_ref[0]
    n_keys = past_len + PAGE_SIZE
    t_row = _imod(lax.broadcasted_iota(jnp.int32, (rows, n_keys), 0), dec_seq)
    j = lax.broadcasted_iota(jnp.int32, (rows, n_keys), 1)
    chosen = _dot(sel_ref[0].astype(BF16), e_ref[...])
    mask = (chosen > 0.5) & (j - past_len <= t_row)
    p_s = _masked_softmax(_dot_nt(q, kall[...]) * SCALE, mask)
    o_s = _diag_heads(_dot(p_s.astype(BF16), vall[...]), NSA_KV_HEADS, rows // NSA_KV_HEADS)

    k = jnp.concatenate([_gather_heads(kw_ref.at[0], NSA_KV_HEADS), kwn_ref[0]], axis=0)
    v = jnp.concatenate([_gather_heads(vw_ref.at[0], NSA_KV_HEADS), vwn_ref[0]], axis=0)
    n_buf = kw_ref.shape[1] // NSA_KV_HEADS
    n_wkeys = k.shape[0]
    t_row = _imod(lax.broadcasted_iota(jnp.int32, (rows, n_wkeys), 0), dec_seq)
    off = lax.broadcasted_iota(jnp.int32, (rows, n_wkeys), 1) - n_buf
    mask_w = (off <= t_row) & (off > t_row - WINDOW) & (off < dec_seq)
    p_w = _masked_softmax(_dot_nt(q, k) * SCALE, mask_w)
    o_w = _diag_heads(_dot(p_w.astype(BF16), v), NSA_KV_HEADS, rows // NSA_KV_HEADS)

    g = g_ref[0]
    o_ref[0] = (g[:, 0:1] * oc_ref[0] + g[:, 1:2] * o_s + g[:, 2:3] * o_w).astype(o_ref.dtype)


def block_to_key_matrix(n_keys):
    blk = np.arange(LANES)[:, None]
    key = np.arange(n_keys)[None, :]
    return jnp.asarray((blk == key // SLC_LEN).astype(np.float32)).astype(BF16)


def nsa_tail_sample(page_table, q_bd, k_pool, v_pool, k_new, v_new, kw, vw, kw_new, vw_new, o_c, sel, gates, dec_seq):
    b, rows, w = q_bd.shape
    n_pages = page_table.shape[1]
    n_keys = (n_pages + 1) * PAGE_SIZE
    pages = _page_specs(n_pages, PAGE_SIZE * NSA_KV_HEADS)
    e_mat = block_to_key_matrix(n_keys)
    per_b = lambda a: pl.BlockSpec((1, *a.shape[1:]), lambda bi, pt: (bi, 0, 0))
    grid_spec = pltpu.PrefetchScalarGridSpec(
        num_scalar_prefetch=1,
        grid=(b,),
        in_specs=[per_b(q_bd), *pages, *pages,
                  *[per_b(a) for a in (k_new, v_new, kw, vw, kw_new, vw_new, o_c, sel, gates)],
                  pl.BlockSpec(e_mat.shape, lambda bi, pt: (0, 0))],
        out_specs=pl.BlockSpec((1, rows, HEAD_DIM), lambda bi, pt: (bi, 0, 0)),
        scratch_shapes=[pltpu.VMEM((n_keys, w), BF16), pltpu.VMEM((n_keys, w), BF16)],
    )
    return pl.pallas_call(
        functools.partial(_nsa_tail_sample_kernel, n_pages=n_pages, dec_seq=dec_seq),
        grid_spec=grid_spec,
        out_shape=jax.ShapeDtypeStruct((b, rows, HEAD_DIM), BF16),
        compiler_params=_cparams(("parallel",)),
        name="nsa_tail_sample",
    )(page_table, q_bd, *([k_pool] * n_pages), *([v_pool] * n_pages), k_new, v_new, kw, vw, kw_new, vw_new,
      o_c, sel, gates, e_mat)


def prep_weights(w_in, w_branch_a, w_branch_b, w_out, w_ffn_gate, w_ffn_up, w_ffn_down, sb_w, nsa_w, kv_w, d_model):
    n_gate = 3 * (nsa_w // HEAD_DIM)
    widths = (sb_w, sb_w, sb_w, nsa_w, kv_w, kv_w, kv_w, kv_w, kv_w, kv_w, n_gate, d_model, d_model)
    offs = np.concatenate([[0], np.cumsum(widths)])
    parts = [w_in[:, int(offs[i]):int(offs[i + 1])].astype(BF16) for i in range(len(widths))]
    parts[10] = jnp.pad(parts[10], ((0, 0), (0, LANES - n_gate)))
    return dict(w_in=parts, wa=w_branch_a.astype(BF16), wb=w_branch_b.astype(BF16), w_out=w_out.astype(BF16),
                wg=w_ffn_gate.astype(BF16), wu=w_ffn_up.astype(BF16), wd=w_ffn_down.astype(BF16))


def projections(x2d, g_mix, w_parts, tables, table_rows):
    xn = rmsnorm(x2d, g_mix, BF16)
    rope = lambda w, dt: mm(xn, w, dt, epi="rope", extras=tables, table_rows=table_rows)
    plain = lambda w, dt: mm(xn, w, dt)
    q_sb, k_sb, v_sb = plain(w_parts[0], BF16), plain(w_parts[1], F32), plain(w_parts[2], F32)
    q_n = rope(w_parts[3], BF16)
    k_c, v_c = rope(w_parts[4], F32), plain(w_parts[5], F32)
    k_s, v_s = rope(w_parts[6], F32), plain(w_parts[7], F32)
    k_w, v_w = rope(w_parts[8], F32), plain(w_parts[9], F32)
    g_n = mm(xn, w_parts[10], F32, epi="sigmoid")
    g_a, g_b = plain(w_parts[11], F32), plain(w_parts[12], F32)
    return q_sb, k_sb, v_sb, q_n, k_c, v_c, k_s, v_s, k_w, v_w, g_n, g_a, g_b


def merge_and_ffn(x2d, o_sb, o_n, g_a, g_b, wts, g_ffn, g_final):
    mixed = merge(o_sb, o_n, wts["wa"], wts["wb"], g_a, g_b)
    h = mm(mixed, wts["w_out"], F32, epi="residual", extras=(x2d,))
    hn = rmsnorm(h, g_ffn, BF16)
    act = ffn_up(hn, wts["wg"], wts["wu"])
    y = mm(act, wts["wd"], F32, epi="residual", extras=(h,), tm=512, tn=256)
    return rmsnorm(y, g_final, F32)


def prompt_layer(x, wts, cw_k, cw_v, g_mix, g_ffn, g_final):
    b, t, d = x.shape
    x2d = x.reshape(b * t, d)
    tables = rope_tables(jnp.arange(t))
    (q_sb, k_sb, v_sb, q_n, k_c, v_c, k_s, v_s, k_w, v_w, g_n, g_a, g_b) = projections(
        x2d, g_mix, wts["w_in"], tables, t)
    r3 = lambda a: a.reshape(b, t, -1)
    n_sb_heads = q_sb.shape[1] // HEAD_DIM
    o_sb = sb_prompt(r3(q_sb), r3(k_sb), r3(v_sb), n_sb_heads)

    n_chunk = t // CMP_STRIDE
    ck = compress_prompt(r3(k_c), *cw_k).reshape(b, NSA_KV_HEADS, n_chunk, HEAD_DIM)
    cv = compress_prompt(r3(v_c), *cw_v).reshape(b, NSA_KV_HEADS, n_chunk, HEAD_DIM)
    n_gate = 3 * NSA_GROUP
    gates = g_n[:, :NSA_KV_HEADS * n_gate].reshape(b, t, NSA_KV_HEADS, n_gate).transpose(0, 2, 1, 3)
    gates = jnp.pad(gates, ((0, 0), (0, 0), (0, 0), (0, LANES - n_gate)))
    m_mat = cmp_to_slc_matrix(n_chunk, n_chunk - CMP_LEN // CMP_STRIDE + 1, t // SLC_LEN)
    o_n = nsa_prompt(r3(q_n), ck, cv, r3(k_s), r3(v_s), r3(k_w), r3(v_w), gates, m_mat)

    y = merge_and_ffn(x2d, o_sb.reshape(b * t, -1), o_n.reshape(b * t, -1), g_a, g_b, wts, g_ffn, g_final)
    heads = lambda a: a.reshape(1, b, t, -1, HEAD_DIM)
    keep = min(WINDOW, t)
    states = (heads(k_sb), heads(v_sb), heads(k_c), heads(v_c), heads(k_s), heads(v_s),
              heads(k_w)[:, :, t - keep:], heads(v_w)[:, :, t - keep:])
    return y.reshape(b, t, d), states


def _block_diag_rows(q, n_heads):
    b, r = q.shape[:2]
    qt = q.transpose(0, 2, 1, 3)
    eye = jnp.eye(n_heads, dtype=q.dtype)
    return (qt[:, :, :, None, :] * eye[None, :, None, :, None]).reshape(b, n_heads * r, n_heads * HEAD_DIM)


def _pad_new(a, b, t):
    return jnp.pad(a.reshape(b, t, -1).astype(BF16), ((0, 0), (0, PAGE_SIZE - t), (0, 0)))


def sample_layer(x, page_table, caches, wts, cw_k, cw_v, g_mix, g_ffn, g_final):
    c_sb_k, c_sb_v, c_cmp_k, c_cmp_v, c_slc_k, c_slc_v, w_buf_k, w_buf_v = caches
    b, t, d = x.shape
    n_pages = page_table.shape[1]
    past_len = n_pages * PAGE_SIZE
    x2d = x.reshape(b * t, d)
    tables = rope_tables(past_len + jnp.arange(b * t) % t)
    (q_sb, k_sb, v_sb, q_n, k_c, v_c, k_s, v_s, k_w, v_w, g_n, g_a, g_b) = projections(
        x2d, g_mix, wts["w_in"], tables, b * t)
    pool_rows = lambda a: a.reshape(a.shape[0], -1, HEAD_DIM)

    n_sb_heads = q_sb.shape[1] // HEAD_DIM
    q_bd = _block_diag_rows(q_sb.reshape(b, t, n_sb_heads, HEAD_DIM), n_sb_heads)
    o_sb = sb_sample(page_table, q_bd, pool_rows(c_sb_k), pool_rows(c_sb_v), _pad_new(k_sb, b, t),
                     _pad_new(v_sb, b, t), n_sb_heads, t)
    o_sb = o_sb.reshape(b, n_sb_heads, t, HEAD_DIM).transpose(0, 2, 1, 3).reshape(b * t, -1)

    qn5 = q_n.reshape(b, t, NSA_KV_HEADS, NSA_GROUP, HEAD_DIM).transpose(0, 2, 3, 1, 4)
    rows = NSA_KV_HEADS * NSA_GROUP * t
    q_rows = qn5.reshape(b, rows, HEAD_DIM)
    qn_bd = _block_diag_rows(qn5.reshape(b, NSA_KV_HEADS, NSA_GROUP * t, HEAD_DIM).transpose(0, 2, 1, 3), NSA_KV_HEADS)
    n_chunk = past_len // CMP_STRIDE
    n_slc = -(-(past_len + t) // SLC_LEN)
    m_mat = cmp_to_slc_matrix(n_chunk, n_chunk - CMP_LEN // CMP_STRIDE + 1, n_slc)
    o_c, sel = cmp_sample(page_table, q_rows, pool_rows(c_cmp_k), pool_rows(c_cmp_v), cw_k, cw_v, m_mat, t)
    gates = g_n[:, :rows // t * 3].reshape(b, t, NSA_KV_HEADS, NSA_GROUP, 3).transpose(0, 2, 3, 1, 4)
    gates = jnp.pad(gates.reshape(b, rows, 3), ((0, 0), (0, 0), (0, LANES - 3)))
    o_n = nsa_tail_sample(page_table, qn_bd, pool_rows(c_slc_k), pool_rows(c_slc_v), _pad_new(k_s, b, t),
                          _pad_new(v_s, b, t), pool_rows(w_buf_k), pool_rows(w_buf_v), _pad_new(k_w, b, t),
                          _pad_new(v_w, b, t), o_c, sel, gates, t)
    o_n = o_n.reshape(b, NSA_KV_HEADS, NSA_GROUP, t, HEAD_DIM).transpose(0, 3, 1, 2, 4).reshape(b * t, -1)

    y = merge_and_ffn(x2d, o_sb, o_n, g_a, g_b, wts, g_ffn, g_final)
    heads = lambda a: a.reshape(1, b, t, -1, HEAD_DIM)
    wk = jnp.concatenate([w_buf_k[None], heads(k_w)], axis=2)[:, :, t:]
    wv = jnp.concatenate([w_buf_v[None], heads(v_w)], axis=2)[:, :, t:]
    states = (heads(k_sb), heads(v_sb), heads(k_c), heads(v_c), heads(k_s), heads(v_s), wk, wv)
    return y.reshape(b, t, d), states


def kernel(x_prompt, x_sample, cache_sb_k, cache_sb_v, cache_cmp_k, cache_cmp_v, cache_slc_k, cache_slc_v,
           state_win_k, state_win_v, page_table, g_mix, w_in, w_cmp_k1, w_cmp_k2, pe_cmp_k, w_cmp_v1, w_cmp_v2,
           pe_cmp_v, w_branch_a, w_branch_b, w_out, g_ffn, w_ffn_gate, w_ffn_up, w_ffn_down, g_final):
    assert w_in.shape[0] == 1, "single-layer trunk"
    d_model = x_prompt.shape[-1]
    sb_w = cache_sb_k.shape[3] * HEAD_DIM
    kv_w = NSA_KV_HEADS * HEAD_DIM
    nsa_w = NSA_KV_HEADS * NSA_GROUP * HEAD_DIM
    wts = prep_weights(w_in[0], w_branch_a[0], w_branch_b[0], w_out[0], w_ffn_gate[0], w_ffn_up[0],
                       w_ffn_down[0], sb_w, nsa_w, kv_w, d_model)
    cw_k = prep_compress_weights(w_cmp_k1[0], pe_cmp_k[0], w_cmp_k2[0])
    cw_v = prep_compress_weights(w_cmp_v1[0], pe_cmp_v[0], w_cmp_v2[0])
    y_p, st_p = prompt_layer(x_prompt, wts, cw_k, cw_v, g_mix[0], g_ffn[0], g_final)
    caches = (cache_sb_k[0], cache_sb_v[0], cache_cmp_k[0], cache_cmp_v[0], cache_slc_k[0], cache_slc_v[0],
              state_win_k[0], state_win_v[0])
    y_s, st_s = sample_layer(x_sample, page_table, caches, wts, cw_k, cw_v, g_mix[0], g_ffn[0], g_final)
    return (y_p, y_s, *st_p, *st_s)
```

```python
import functools

import jax
import jax.numpy as jnp
import numpy as np
from jax import lax
from jax.experimental import pallas as pl
from jax.experimental.pallas import tpu as pltpu

F32 = jnp.float32
BF16 = jnp.bfloat16

HEAD_DIM = 128
LANES = 128
NSA_KV_HEADS = 4
NSA_GROUP = 4
CMP_LEN = 32
CMP_STRIDE = 16
CMP_HID = 2 * HEAD_DIM
SLC_LEN = 64
N_SEL = 16
WINDOW = 512
ROPE_THETA = 500000.0
ROPE_DIM = HEAD_DIM // 4
ROPE_HALF = ROPE_DIM // 2
RMS_EPS = 1e-6
NEG = -1e30
PAGE_SIZE = 128
SCALE = HEAD_DIM ** -0.5
LOG2E = 1.4426950408889634
VMEM_LIMIT = 56 * 1024 * 1024


def _cparams(sem):
    return pltpu.CompilerParams(dimension_semantics=sem, vmem_limit_bytes=VMEM_LIMIT)


def _dot(a, b):
    return jnp.dot(a, b, preferred_element_type=F32)


def _dot_nt(a, b):
    return lax.dot_general(a, b, (((1,), (1,)), ((), ())), preferred_element_type=F32)


def _dot_split(x, m_bf16):
    hi = x.astype(BF16)
    lo = (x - hi.astype(F32)).astype(BF16)
    return _dot(hi, m_bf16) + _dot(lo, m_bf16)


def _dot_split_left(m_bf16, x):
    hi = x.astype(BF16)
    lo = (x - hi.astype(F32)).astype(BF16)
    return _dot(m_bf16, hi) + _dot(m_bf16, lo)


def _idiv(x, d):
    return x >> (d.bit_length() - 1) if d & (d - 1) == 0 else x // d


def _imod(x, d):
    return x & (d - 1) if d & (d - 1) == 0 else x % d


def _softplus(z):
    return jnp.maximum(z, 0.0) + jnp.log(1.0 + jnp.exp(-jnp.abs(z)))


def _rmsnorm_kernel(x_ref, g_ref, o_ref):
    x = x_ref[...]
    y = x * lax.rsqrt(jnp.mean(x * x, axis=-1, keepdims=True) + RMS_EPS)
    o_ref[...] = (y * g_ref[...]).astype(o_ref.dtype)


def rmsnorm(x, g, out_dtype, tm=512):
    m, d = x.shape
    return pl.pallas_call(
        _rmsnorm_kernel,
        grid=(m // tm,),
        in_specs=[pl.BlockSpec((tm, d), lambda i: (i, 0)),
                  pl.BlockSpec((1, d), lambda i: (0, 0))],
        out_specs=pl.BlockSpec((tm, d), lambda i: (i, 0)),
        out_shape=jax.ShapeDtypeStruct((m, d), out_dtype),
        compiler_params=_cparams(("parallel",)),
        name="rmsnorm",
    )(x, g.reshape(1, d))


def _rope_tile(acc, c, s1, s2):
    return (acc * c + pltpu.roll(acc, LANES - ROPE_HALF, axis=1) * s1
            + pltpu.roll(acc, ROPE_HALF, axis=1) * s2)


def _mm_kernel(*refs, epi):
    if epi == "rope":
        x_ref, w_ref, c_ref, s1_ref, s2_ref, o_ref = refs
    elif epi == "residual":
        x_ref, w_ref, r_ref, o_ref = refs
    else:
        x_ref, w_ref, o_ref = refs
    acc = _dot(x_ref[...], w_ref[...])
    if epi == "rope":
        c, s1, s2 = c_ref[...], s1_ref[...], s2_ref[...]
        for hh in range(acc.shape[1] // LANES):
            sl = slice(hh * LANES, (hh + 1) * LANES)
            o_ref[:, sl] = _rope_tile(acc[:, sl], c, s1, s2).astype(o_ref.dtype)
    elif epi == "sigmoid":
        o_ref[...] = jax.nn.sigmoid(acc).astype(o_ref.dtype)
    elif epi == "residual":
        o_ref[...] = (r_ref[...] + acc).astype(o_ref.dtype)
    else:
        o_ref[...] = acc.astype(o_ref.dtype)


def mm(x, w, out_dtype, epi="plain", extras=(), tm=1024, tn=512, table_rows=None):
    m, k = x.shape
    col0 = 0
    if isinstance(w, tuple):
        w, col0, n = w
    else:
        n = w.shape[1]
    tm, tn = min(tm, m), min(tn, n)
    assert col0 % tn == 0 and n % tn == 0
    jb = col0 // tn
    in_specs = [pl.BlockSpec((tm, k), lambda i, j: (i, 0)),
                pl.BlockSpec((k, tn), lambda i, j: (0, jb + j))]
    if epi == "rope":
        nblk = table_rows // tm
        in_specs += [pl.BlockSpec((tm, LANES), lambda i, j: (i % nblk, 0))] * 3
    elif epi == "residual":
        in_specs += [pl.BlockSpec((tm, tn), lambda i, j: (i, j))]
    return pl.pallas_call(
        functools.partial(_mm_kernel, epi=epi),
        grid=(m // tm, n // tn),
        in_specs=in_specs,
        out_specs=pl.BlockSpec((tm, tn), lambda i, j: (i, j)),
        out_shape=jax.ShapeDtypeStruct((m, n), out_dtype),
        compiler_params=_cparams(("parallel", "arbitrary")),
        name="mm_" + epi,
    )(x, w, *extras)


def _merge_kernel(a_ref, b_ref, wa_ref, wb_ref, ga_ref, gb_ref, o_ref):
    ya = _dot(a_ref[...], wa_ref[...])
    yb = _dot(b_ref[...], wb_ref[...])
    o_ref[...] = (jax.nn.sigmoid(ga_ref[...]) * ya + jax.nn.sigmoid(gb_ref[...]) * yb).astype(o_ref.dtype)


def merge(o_a, o_b, wa, wb, g_a, g_b, tm=1024, tn=512):
    m, k = o_a.shape
    n = wa.shape[1]
    tm = min(tm, m)
    xs = pl.BlockSpec((tm, k), lambda i, j: (i, 0))
    ws = pl.BlockSpec((k, tn), lambda i, j: (0, j))
    gs = pl.BlockSpec((tm, tn), lambda i, j: (i, j))
    return pl.pallas_call(
        _merge_kernel,
        grid=(m // tm, n // tn),
        in_specs=[xs, xs, ws, ws, gs, gs],
        out_specs=gs,
        out_shape=jax.ShapeDtypeStruct((m, n), BF16),
        compiler_params=_cparams(("parallel", "arbitrary")),
        name="merge",
    )(o_a, o_b, wa, wb, g_a, g_b)


def _ffn_up_kernel(x_ref, wg_ref, wu_ref, o_ref):
    x = x_ref[...]
    a = _dot(x, wg_ref[...])
    b = _dot(x, wu_ref[...])
    o_ref[...] = (a * jax.nn.sigmoid(a) * b).astype(o_ref.dtype)


def ffn_up(x, wg, wu, tm=1024, tn=256):
    m, k = x.shape
    n = wg.shape[1]
    tm = min(tm, m)
    ws = pl.BlockSpec((k, tn), lambda i, j: (0, j))
    return pl.pallas_call(
        _ffn_up_kernel,
        grid=(m // tm, n // tn),
        in_specs=[pl.BlockSpec((tm, k), lambda i, j: (i, 0)), ws, ws],
        out_specs=pl.BlockSpec((tm, tn), lambda i, j: (i, j)),
        out_shape=jax.ShapeDtypeStruct((m, n), BF16),
        compiler_params=_cparams(("parallel", "arbitrary")),
        name="ffn_up",
    )(x, wg, wu)


def rope_tables(pos):
    inv = jnp.float32(ROPE_THETA) ** (-(jnp.arange(ROPE_HALF, dtype=F32) * 2.0 / ROPE_DIM))
    ang = pos.astype(F32)[:, None] * inv[None, :]
    cos, sin = jnp.cos(ang), jnp.sin(ang)
    n = pos.shape[0]
    rest = HEAD_DIM - ROPE_DIM
    c = jnp.concatenate([cos, cos, jnp.ones((n, rest), F32)], axis=1)
    s1 = jnp.concatenate([-sin, jnp.zeros((n, HEAD_DIM - ROPE_HALF), F32)], axis=1)
    s2 = jnp.concatenate([jnp.zeros((n, ROPE_HALF), F32), sin, jnp.zeros((n, rest), F32)], axis=1)
    return c, s1, s2


SB_BLK = 256
SB_HEADS_PER_STEP = 4


def _strict_lower(n):
    r = lax.broadcasted_iota(jnp.int32, (n, n), 0)
    c = lax.broadcasted_iota(jnp.int32, (n, n), 1)
    return jnp.where(r > c, 1.0, 0.0).astype(BF16)


def _sb_tiles(qs, ks, vs, tri, carry_in, mask, chained, expand=None):
    zs = [_dot_nt(q, k) * SCALE for q, k in zip(qs, ks)]
    pre, lgs, his, los = [], [], [], []
    for z in zs:
        sp = _softplus(z)
        lg = -sp if mask is None else jnp.where(mask, -sp, 0.0)
        hi = lg.astype(BF16)
        pre.append(z - sp)
        lgs.append(lg)
        his.append(hi)
        los.append((lg - hi.astype(F32)).astype(BF16))
    cs_hi = [_dot(hi, tri) for hi in his]
    cs_lo = [_dot(lo, tri) for lo in los]
    carries, probs = [], []
    carry = carry_in
    for i, lg in enumerate(lgs):
        c0 = carry if chained else carry_in[i]
        a = jnp.exp(pre[i] + (cs_hi[i] + cs_lo[i]) + c0)
        if mask is not None:
            a = jnp.where(mask, a, 0.0)
        probs.append(a.astype(BF16))
        carry = c0 + jnp.sum(lg, axis=1, keepdims=True)
        carries.append(carry)
    if expand is not None:
        spread, own_head = expand
        probs = [_dot(a, spread).astype(BF16) * own_head for a in probs]
    return [_dot(a, v) for a, v in zip(probs, vs)], carries


def _sb_tile(q, k, v, tri, carry, mask):
    outs, carries = _sb_tiles([q], [k], [v], tri, [carry], mask, False)
    return outs[0], carries[0]


def _sb_prompt_kernel(q_ref, k_ref, v_ref, o_ref, kb_ref, vb_ref):
    t = q_ref.shape[1]
    kb_ref[...] = k_ref[0].astype(BF16)
    vb_ref[...] = v_ref[0].astype(BF16)
    tri = _strict_lower(SB_BLK)
    r = lax.broadcasted_iota(jnp.int32, (SB_BLK, SB_BLK), 0)
    c = lax.broadcasted_iota(jnp.int32, (SB_BLK, SB_BLK), 1)
    diag_mask = c < r
    heads = [slice(h * HEAD_DIM, (h + 1) * HEAD_DIM) for h in range(SB_HEADS_PER_STEP)]

    def q_block(qi, _):
        q0 = pl.multiple_of(qi * SB_BLK, SB_BLK)
        qs = [q_ref[0, pl.ds(q0, SB_BLK), hs] for hs in heads]

        def tiles(k0, carries, mask):
            return _sb_tiles(qs, [kb_ref[pl.ds(k0, SB_BLK), hs] for hs in heads],
                             [vb_ref[pl.ds(k0, SB_BLK), hs] for hs in heads], tri, carries, mask, False)

        accs, carries = tiles(q0, [jnp.zeros((SB_BLK, 1), F32)] * len(heads), diag_mask)

        def k_block(step, st):
            accs, carries = st
            outs, carries = tiles(pl.multiple_of((qi - 1 - step) * SB_BLK, SB_BLK), carries, None)
            return tuple(a + o for a, o in zip(accs, outs)), tuple(carries)

        accs, _ = lax.fori_loop(0, qi, k_block, (tuple(accs), tuple(carries)))
        for hs, acc in zip(heads, accs):
            o_ref[0, pl.ds(q0, SB_BLK), hs] = acc.astype(o_ref.dtype)
        return 0

    lax.fori_loop(0, t // SB_BLK, q_block, 0)


def sb_prompt(q, k, v, n_heads):
    b, t, _ = q.shape
    w = SB_HEADS_PER_STEP * HEAD_DIM
    spec = pl.BlockSpec((1, t, w), lambda bi, h: (bi, 0, h))
    return pl.pallas_call(
        _sb_prompt_kernel,
        grid=(b, n_heads // SB_HEADS_PER_STEP),
        in_specs=[spec, spec, spec],
        out_specs=spec,
        out_shape=jax.ShapeDtypeStruct(q.shape, BF16),
        scratch_shapes=[pltpu.VMEM((t, w), BF16), pltpu.VMEM((t, w), BF16)],
        compiler_params=_cparams(("parallel", "parallel")),
        name="sb_prompt",
    )(q, k, v)


CMP_PITCH = 20


def _split_heads(x_ref, xh_ref, chunk0):
    n = x_ref.shape[0] // NSA_KV_HEADS
    for c in range(n // CMP_STRIDE):
        for h in range(NSA_KV_HEADS):
            r0 = (chunk0 + c) * CMP_PITCH
            xh_ref[h, r0:r0 + CMP_STRIDE, :] = x_ref[
                pl.ds(c * CMP_STRIDE * NSA_KV_HEADS + h, CMP_STRIDE, stride=NSA_KV_HEADS), :]


def _compress_heads(xh_refs, weights):
    n_chunk = xh_refs[0].shape[1] // CMP_PITCH
    rows = NSA_KV_HEADS * n_chunk
    lhss = [jnp.concatenate(
        [jnp.concatenate([xh_ref[h, pl.ds(s, n_chunk, stride=CMP_PITCH), :].astype(BF16)
                          for s in range(CMP_STRIDE)], axis=1)
         for h in range(NSA_KV_HEADS)], axis=0) for xh_ref in xh_refs]
    accs = [_dot(lhs, w1_ref[...]) for lhs, (w1_ref, _, _) in zip(lhss, weights)]
    hids = []
    for acc, (_, bias_ref, _) in zip(accs, weights):
        hid = acc[:, :CMP_HID] + pltpu.roll(acc[:, CMP_HID:], rows - 1, axis=0) + bias_ref[0:1, :]
        hids.append((hid * jax.nn.sigmoid(hid)).astype(BF16))
    outs = [_dot(hid, w2_ref[...]) for hid, (_, _, w2_ref) in zip(hids, weights)]
    chunk = _imod(lax.broadcasted_iota(jnp.int32, (rows, HEAD_DIM), 0), n_chunk)
    return [jnp.where(chunk < n_chunk - 1, out, 0.0) for out in outs]


def _compress_kernel(x_ref, w1_ref, bias_ref, w2_ref, o_ref, xh_ref):
    _split_heads(x_ref.at[0], xh_ref, 0)
    o_ref[0] = _compress_heads([xh_ref], [(w1_ref, bias_ref, w2_ref)])[0].astype(o_ref.dtype)


def compress_prompt(x, w1r, bias, w2):
    b, t, _ = x.shape
    rows = NSA_KV_HEADS * (t // CMP_STRIDE)
    x = x.reshape(b, t * NSA_KV_HEADS, HEAD_DIM)
    return pl.pallas_call(
        _compress_kernel,
        grid=(b,),
        in_specs=[pl.BlockSpec((1, t * NSA_KV_HEADS, HEAD_DIM), lambda i: (i, 0, 0)),
                  pl.BlockSpec(w1r.shape, lambda i: (0, 0)),
                  pl.BlockSpec(bias.shape, lambda i: (0, 0)),
                  pl.BlockSpec(w2.shape, lambda i: (0, 0))],
        out_specs=pl.BlockSpec((1, rows, HEAD_DIM), lambda i: (i, 0, 0)),
        out_shape=jax.ShapeDtypeStruct((b, rows, HEAD_DIM), BF16),
        scratch_shapes=[pltpu.VMEM((NSA_KV_HEADS, t // CMP_STRIDE * CMP_PITCH, HEAD_DIM), F32)],
        compiler_params=_cparams(("parallel",)),
        name="compress_prompt",
    )(x, w1r, bias, w2)


def prep_compress_weights(w1, pe, w2):
    r = CMP_LEN // CMP_STRIDE
    w1r = w1.reshape(r, CMP_STRIDE, HEAD_DIM, CMP_HID).transpose(1, 2, 0, 3)
    w1r = w1r.reshape(CMP_STRIDE * HEAD_DIM, r * CMP_HID).astype(BF16)
    pe_rows = jnp.zeros((16, CMP_LEN * HEAD_DIM), F32).at[0].set(pe.reshape(-1)).astype(BF16)
    bias = mm(pe_rows, w1.reshape(CMP_LEN * HEAD_DIM, CMP_HID).astype(BF16), F32, tn=CMP_HID)
    return w1r, bias, w2.astype(BF16)


def cmp_to_slc_matrix(n_cmp_pad, n_cmp, n_slc):
    i = np.arange(n_cmp_pad)[:, None]
    j = np.arange(LANES)[None, :]
    lo = np.maximum(i * CMP_STRIDE, j * SLC_LEN)
    hi = np.minimum(i * CMP_STRIDE + CMP_LEN, j * SLC_LEN + SLC_LEN)
    m = np.clip(hi - lo, 0, None) / CMP_LEN
    m = np.where((i < n_cmp) & (j < n_slc), m, 0.0)
    return jnp.asarray(m.astype(np.float32)).astype(BF16)


def _masked_softmax(s, mask):
    s = jnp.where(mask, s, NEG)
    m = jnp.max(s, axis=1, keepdims=True)
    e = jnp.where(mask, jnp.exp(s - m), 0.0)
    den = jnp.sum(e, axis=1, keepdims=True)
    return e / jnp.where(den > 0.0, den, 1.0)


def _select_blocks(imp, qpos, n_slc):
    blk = lax.broadcasted_iota(jnp.int32, imp.shape, 1)
    cur = _idiv(qpos, SLC_LEN)
    valid = (blk * SLC_LEN <= qpos) & (blk < n_slc)
    forced = (blk == 0) | (blk == cur) | (blk == cur - 1)
    score = jnp.where(valid, jnp.where(forced, jnp.inf, imp), -jnp.inf)
    rank = jnp.zeros(imp.shape, F32)
    for i in range(n_slc):
        col = score[:, i:i + 1]
        ge = jnp.where(col >= score, 1.0, 0.0)
        gt = jnp.where(col > score, 1.0, 0.0)
        rank = rank + jnp.where(blk > i, ge, gt)
    return jnp.where(valid & (rank < float(min(N_SEL, n_slc))), 1.0, 0.0)


def _select_blocks_t(imp_t, qpos_row, n_slc):
    blk = lax.broadcasted_iota(jnp.int32, imp_t.shape, 0)
    cur = _idiv(qpos_row, SLC_LEN)
    valid = blk * SLC_LEN <= qpos_row
    forced = (blk == 0) | (blk == cur) | (blk == cur - 1)
    score = jnp.where(valid, jnp.where(forced, jnp.inf, imp_t), -jnp.inf)
    rank = jnp.zeros(imp_t.shape, F32)
    for i in range(n_slc):
        row = score[i:i + 1, :]
        ge = jnp.where(row >= score, 1.0, 0.0)
        gt = jnp.where(row > score, 1.0, 0.0)
        rank = rank + jnp.where(blk > i, ge, gt)
    return jnp.where(valid & (rank < float(min(N_SEL, n_slc))), 1.0, 0.0)


NSA_QB = 128
NSA_KB = 512


def _nsa_prompt_kernel(q_ref, ck_ref, cv_ref, ks_ref, vs_ref, kw_ref, vw_ref, g_ref, mt_ref, o_ref,
                       ksb, vsb, kwb, vwb):
    t = ks_ref.shape[1]
    qb = pl.program_id(2)
    n_slc = t // SLC_LEN

    @pl.when(qb == 0)
    def _():
        ksb[...] = ks_ref[0].astype(BF16)
        vsb[...] = vs_ref[0].astype(BF16)
        kwb[...] = kw_ref[0].astype(BF16)
        vwb[...] = vw_ref[0].astype(BF16)

    rows = NSA_GROUP * NSA_QB
    q0 = qb * NSA_QB
    q = jnp.concatenate([q_ref[0, :, g * HEAD_DIM:(g + 1) * HEAD_DIM] for g in range(NSA_GROUP)], axis=0)
    qpos1 = q0 + lax.broadcasted_iota(jnp.int32, (NSA_QB, 1), 0)
    qpos = jnp.concatenate([qpos1] * NSA_GROUP, axis=0)

    tile_rows = lambda a: jnp.concatenate([a] * NSA_GROUP, axis=0)

    n_cmp_pad = ck_ref.shape[2]
    band = WINDOW + NSA_QB
    w0 = pl.multiple_of(jnp.maximum(q0 - WINDOW, 0), NSA_QB)
    s_c = _dot_nt(q, ck_ref[0, 0]) * (SCALE * LOG2E)
    s_w = _dot_nt(q, kwb[pl.ds(w0, band), :]) * (SCALE * LOG2E)

    cmp_end = lax.broadcasted_iota(jnp.int32, (rows, n_cmp_pad), 1) * CMP_STRIDE + (CMP_LEN - 1)
    mask_c = cmp_end <= qpos
    s_c = jnp.where(mask_c, s_c, NEG)
    e_c = jnp.where(mask_c, jnp.exp2(s_c - jnp.max(s_c, axis=1, keepdims=True)), 0.0)
    den_c = jnp.sum(e_c, axis=1, keepdims=True)
    p_c = e_c * (1.0 / jnp.where(den_c > 0.0, den_c, 1.0))
    o_c = _dot(p_c.astype(BF16), cv_ref[0, 0])

    dist = qpos1 - (w0 + lax.broadcasted_iota(jnp.int32, (NSA_QB, band), 1))
    s_w = s_w + tile_rows(jnp.where((dist >= 0) & (dist < WINDOW), 0.0, NEG))
    e_w = jnp.exp2(s_w - jnp.max(s_w, axis=1, keepdims=True))
    p_w = e_w * (1.0 / jnp.sum(e_w, axis=1, keepdims=True))
    o_w = _dot(p_w.astype(BF16), vwb[pl.ds(w0, band), :])

    p_sum = p_c[0:NSA_QB]
    for g in range(1, NSA_GROUP):
        p_sum = p_sum + p_c[g * NSA_QB:(g + 1) * NSA_QB]
    p_hi = p_sum.astype(BF16)
    p_lo = (p_sum - p_hi.astype(F32)).astype(BF16)
    imp_t = _dot_nt(mt_ref[...], p_hi) + _dot_nt(mt_ref[...], p_lo)
    qpos_row = q0 + lax.broadcasted_iota(jnp.int32, (1, NSA_QB), 1)
    sel_t = _select_blocks_t(imp_t, qpos_row, n_slc).astype(BF16)
    eye = jnp.where(lax.broadcasted_iota(jnp.int32, (NSA_QB, NSA_QB), 0)
                    == lax.broadcasted_iota(jnp.int32, (NSA_QB, NSA_QB), 1), 1.0, 0.0).astype(BF16)
    sel = _dot_nt(eye, sel_t).astype(BF16)

    def slc_chunk(kc, st):
        m_i, l_i, acc = st
        k0 = pl.multiple_of(kc * NSA_KB, NSA_KB)
        blk_of_key = _idiv(k0 + lax.broadcasted_iota(jnp.int32, (n_slc, NSA_KB), 1), SLC_LEN)
        expand = jnp.where(lax.broadcasted_iota(jnp.int32, (n_slc, NSA_KB), 0) == blk_of_key, 1.0, 0.0)
        chosen = _dot(sel, expand.astype(BF16))
        kpos = k0 + lax.broadcasted_iota(jnp.int32, (NSA_QB, NSA_KB), 1)
        bias = tile_rows(jnp.where((chosen > 0.5) & (kpos <= qpos1), 0.0, NEG))
        s = _dot_nt(q, ksb[pl.ds(k0, NSA_KB), :]) * (SCALE * LOG2E) + bias
        m_new = jnp.maximum(m_i, jnp.max(s, axis=1, keepdims=True))
        alpha = jnp.exp2(m_i - m_new)
        e = jnp.exp2(s - m_new)
        l_new = alpha * l_i + jnp.sum(e, axis=1, keepdims=True)
        acc = alpha * acc + _dot(e.astype(BF16), vsb[pl.ds(k0, NSA_KB), :])
        return m_new, l_new, acc

    n_chunks = (q0 + NSA_QB + NSA_KB - 1) // NSA_KB
    m_i, l_i, acc = lax.fori_loop(
        0, n_chunks, slc_chunk,
        (jnp.full((rows, 1), NEG, F32), jnp.zeros((rows, 1), F32), jnp.zeros((rows, HEAD_DIM), F32)))
    o_s = acc * (1.0 / l_i)

    gates = g_ref[0, 0]
    for g in range(NSA_GROUP):
        rs = slice(g * NSA_QB, (g + 1) * NSA_QB)
        o = (gates[:, 3 * g:3 * g + 1] * o_c[rs] + gates[:, 3 * g + 1:3 * g + 2] * o_s[rs]
             + gates[:, 3 * g + 2:3 * g + 3] * o_w[rs])
        o_ref[0, :, g * HEAD_DIM:(g + 1) * HEAD_DIM] = o.astype(o_ref.dtype)


def nsa_prompt(q, ck, cv, ks, vs, kw, vw, gates, m_mat):
    b, t, _ = q.shape
    m_mat = m_mat[:, :t // SLC_LEN].T
    gw = NSA_GROUP * HEAD_DIM
    qspec = pl.BlockSpec((1, NSA_QB, gw), lambda bi, h, qb: (bi, qb, h))
    cspec = pl.BlockSpec((1, 1, ck.shape[2], HEAD_DIM), lambda bi, h, qb: (bi, h, 0, 0))
    kspec = pl.BlockSpec((1, t, HEAD_DIM), lambda bi, h, qb: (bi, 0, h))
    return pl.pallas_call(
        _nsa_prompt_kernel,
        grid=(b, NSA_KV_HEADS, t // NSA_QB),
        in_specs=[qspec, cspec, cspec, kspec, kspec, kspec, kspec,
                  pl.BlockSpec((1, 1, NSA_QB, LANES), lambda bi, h, qb: (bi, h, qb, 0)),
                  pl.BlockSpec(m_mat.shape, lambda bi, h, qb: (0, 0))],
        out_specs=qspec,
        out_shape=jax.ShapeDtypeStruct(q.shape, BF16),
        scratch_shapes=[pltpu.VMEM((t, HEAD_DIM), BF16)] * 4,
        compiler_params=_cparams(("parallel", "parallel", "arbitrary")),
        name="nsa_prompt",
    )(q, ck, cv, ks, vs, kw, vw, gates, m_mat)


def _diag_heads(acc, n_heads, rows_per_head):
    return jnp.concatenate(
        [acc[h * rows_per_head:(h + 1) * rows_per_head, h * HEAD_DIM:(h + 1) * HEAD_DIM]
         for h in range(n_heads)], axis=0)


def _gather_heads(x_ref, n_heads):
    n = x_ref.shape[0] // n_heads
    return jnp.concatenate([x_ref[pl.ds(h, n, stride=n_heads), :].astype(BF16) for h in range(n_heads)], axis=1)


SB_PAGES_PER_STEP = 8


def _sb_sample_kernel(pt_ref, q_ref, *refs, n_heads, dec_seq):
    g = SB_PAGES_PER_STEP
    kp_refs, vp_refs = refs[:g], refs[g:2 * g]
    kn_ref, vn_ref, spread_ref, own_ref, o_ref, acc_ref, carry_ref = refs[2 * g:]
    p = pl.program_id(1)
    rows = q_ref.shape[1]
    tri = _strict_lower(PAGE_SIZE)

    @pl.when(p == 0)
    def _():
        t_row = _imod(lax.broadcasted_iota(jnp.int32, (rows, PAGE_SIZE), 0), dec_seq)
        j = lax.broadcasted_iota(jnp.int32, (rows, PAGE_SIZE), 1)
        o, carry = _sb_tile(q_ref[0], kn_ref[0], vn_ref[0], tri, jnp.zeros((rows, 1), F32), j < t_row)
        acc_ref[...] = _diag_heads(o, n_heads, dec_seq)
        carry_ref[...] = carry

    q = q_ref[0]
    outs, carries = _sb_tiles([q] * g, [_gather_heads(r.at[0], n_heads) for r in kp_refs],
                              [r[0].astype(BF16) for r in vp_refs], tri, carry_ref[...], None, True,
                              expand=(spread_ref[...], own_ref[...]))
    acc = acc_ref[...]
    for o in outs:
        acc = acc + o
    acc_ref[...] = acc
    carry_ref[...] = carries[-1]

    @pl.when(p == pl.num_programs(1) - 1)
    def _():
        o_ref[0] = acc_ref[...].astype(o_ref.dtype)


def sb_sample(page_table, q_bd, k_pool, v_pool, k_new, v_new, n_heads, dec_seq):
    b, rows, w = q_bd.shape
    n_pages = page_table.shape[1]
    g = SB_PAGES_PER_STEP
    page_rows = PAGE_SIZE * n_heads
    cache_row = np.arange(page_rows)
    spread = jnp.asarray(cache_row[None, :] // n_heads == np.arange(PAGE_SIZE)[:, None], BF16)
    own_head = jnp.asarray(cache_row[None, :] % n_heads == (np.arange(rows) // dec_seq)[:, None], BF16)
    pool_specs = [pl.BlockSpec((1, page_rows, HEAD_DIM),
                               lambda bi, p, pt, i=i: (pt[bi, n_pages - 1 - (p * g + i)], 0, 0)) for i in range(g)]
    new_spec = pl.BlockSpec((1, PAGE_SIZE, w), lambda bi, p, pt: (bi, 0, 0))
    const = lambda a: pl.BlockSpec(a.shape, lambda bi, p, pt: (0, 0))
    grid_spec = pltpu.PrefetchScalarGridSpec(
        num_scalar_prefetch=1,
        grid=(b, n_pages // g),
        in_specs=[pl.BlockSpec((1, rows, w), lambda bi, p, pt: (bi, 0, 0)),
                  *pool_specs, *pool_specs, new_spec, new_spec, const(spread), const(own_head)],
        out_specs=pl.BlockSpec((1, rows, HEAD_DIM), lambda bi, p, pt: (bi, 0, 0)),
        scratch_shapes=[pltpu.VMEM((rows, HEAD_DIM), F32), pltpu.VMEM((rows, 1), F32)],
    )
    return pl.pallas_call(
        functools.partial(_sb_sample_kernel, n_heads=n_heads, dec_seq=dec_seq),
        grid_spec=grid_spec,
        out_shape=jax.ShapeDtypeStruct((b, rows, HEAD_DIM), BF16),
        compiler_params=_cparams(("parallel", "arbitrary")),
        name="sb_sample",
    )(page_table, q_bd, *([k_pool] * g), *([v_pool] * g), k_new, v_new, spread, own_head)


def _cmp_sample_kernel(pt_ref, q_ref, *refs, n_pages, dec_seq):
    kp_refs, vp_refs = refs[:n_pages], refs[n_pages:2 * n_pages]
    (w1k_ref, bk_ref, w2k_ref, w1v_ref, bv_ref, w2v_ref, m_ref, oc_ref, sel_ref, xk_ref, xv_ref) = refs[2 * n_pages:]
    past_len = n_pages * PAGE_SIZE
    n_chunk = past_len // CMP_STRIDE
    n_slc = -(-(past_len + dec_seq) // SLC_LEN)

    for i in range(n_pages):
        _split_heads(kp_refs[i].at[0], xk_ref, i * (PAGE_SIZE // CMP_STRIDE))
        _split_heads(vp_refs[i].at[0], xv_ref, i * (PAGE_SIZE // CMP_STRIDE))
    ck, cv = [c.astype(BF16) for c in _compress_heads(
        [xk_ref, xv_ref], [(w1k_ref, bk_ref, w2k_ref), (w1v_ref, bv_ref, w2v_ref)])]
    rows = NSA_GROUP * dec_seq
    t_row = _imod(lax.broadcasted_iota(jnp.int32, (rows, 1), 0), dec_seq)
    qpos = past_len + t_row
    cmp_end = lax.broadcasted_iota(jnp.int32, (rows, n_chunk), 1) * CMP_STRIDE + (CMP_LEN - 1)
    r = lax.broadcasted_iota(jnp.int32, (rows, rows), 0)
    c = lax.broadcasted_iota(jnp.int32, (rows, rows), 1)
    same_t = jnp.where(_imod(r, dec_seq) == _imod(c, dec_seq), 1.0, 0.0).astype(BF16)
    for h in range(NSA_KV_HEADS):
        q = q_ref[0, h * rows:(h + 1) * rows, :]
        cs = slice(h * n_chunk, (h + 1) * n_chunk)
        p_c = _masked_softmax(_dot_nt(q, ck[cs]) * SCALE, cmp_end <= qpos)
        oc_ref[0, h * rows:(h + 1) * rows, :] = _dot(p_c.astype(BF16), cv[cs])
        p_sum = _dot_split_left(same_t, p_c)
        imp = _dot_split(p_sum, m_ref[...])
        sel_ref[0, h * rows:(h + 1) * rows, :] = _select_blocks(imp, qpos, n_slc)


def _page_specs(n_pages, page_rows):
    return [pl.BlockSpec((1, page_rows, HEAD_DIM), lambda bi, pt, i=i: (pt[bi, i], 0, 0)) for i in range(n_pages)]


def cmp_sample(page_table, q_rows, k_pool, v_pool, wk, wv, m_mat, dec_seq):
    b, rows, _ = q_rows.shape
    n_pages = page_table.shape[1]
    past_len = n_pages * PAGE_SIZE
    n_chunk = past_len // CMP_STRIDE
    pages = _page_specs(n_pages, PAGE_SIZE * NSA_KV_HEADS)
    const = lambda a: pl.BlockSpec(a.shape, lambda bi, pt: (0,) * a.ndim)
    row_spec = pl.BlockSpec((1, rows, HEAD_DIM), lambda bi, pt: (bi, 0, 0))
    grid_spec = pltpu.PrefetchScalarGridSpec(
        num_scalar_prefetch=1,
        grid=(b,),
        in_specs=[row_spec, *pages, *pages, *[const(a) for a in (*wk, *wv, m_mat)]],
        out_specs=[row_spec, row_spec],
        scratch_shapes=[pltpu.VMEM((NSA_KV_HEADS, n_chunk * CMP_PITCH, HEAD_DIM), F32)] * 2,
    )
    return pl.pallas_call(
        functools.partial(_cmp_sample_kernel, n_pages=n_pages, dec_seq=dec_seq),
        grid_spec=grid_spec,
        out_shape=[jax.ShapeDtypeStruct((b, rows, HEAD_DIM), F32)] * 2,
        compiler_params=_cparams(("parallel",)),
        name="cmp_sample",
    )(page_table, q_rows, *([k_pool] * n_pages), *([v_pool] * n_pages), *wk, *wv, m_mat)


def _nsa_tail_sample_kernel(pt_ref, q_ref, *refs, n_pages, dec_seq):
    kp_refs, vp_refs = refs[:n_pages], refs[n_pages:2 * n_pages]
    (kn_ref, vn_ref, kw_ref, vw_ref, kwn_ref, vwn_ref, oc_ref, sel_ref, g_ref, e_ref, o_ref,
     kall, vall) = refs[2 * n_pages:]
    rows = q_ref.shape[1]
    past_len = n_pages * PAGE_SIZE
    q = q_ref[0]

    for i in range(n_pages):
        kall[i * PAGE_SIZE:(i + 1) * PAGE_SIZE, :] = _gather_heads(kp_refs[i].at[0], NSA_KV_HEADS)
        vall[i * PAGE_SIZE:(i + 1) * PAGE_SIZE, :] = _gather_heads(vp_refs[i].at[0], NSA_KV_HEADS)
    kall[past_len:past_len + PAGE_SIZE, :] = kn_ref[0]
    vall[past_len:past_len + PAGE_SIZE, :] = vn_ref[0]
    n_keys = past_len + PAGE_SIZE
    t_row = _imod(lax.broadcasted_iota(jnp.int32, (rows, n_keys), 0), dec_seq)
    j = lax.broadcasted_iota(jnp.int32, (rows, n_keys), 1)
    chosen = _dot(sel_ref[0].astype(BF16), e_ref[...])
    mask = (chosen > 0.5) & (j - past_len <= t_row)
    p_s = _masked_softmax(_dot_nt(q, kall[...]) * SCALE, mask)
    o_s = _diag_heads(_dot(p_s.astype(BF16), vall[...]), NSA_KV_HEADS, rows // NSA_KV_HEADS)

    k = jnp.concatenate([_gather_heads(kw_ref.at[0], NSA_KV_HEADS), kwn_ref[0]], axis=0)
    v = jnp.concatenate([_gather_heads(vw_ref.at[0], NSA_KV_HEADS), vwn_ref[0]], axis=0)
    n_buf = kw_ref.shape[1] // NSA_KV_HEADS
    n_wkeys = k.shape[0]
    t_row = _imod(lax.broadcasted_iota(jnp.int32, (rows, n_wkeys), 0), dec_seq)
    off = lax.broadcasted_iota(jnp.int32, (rows, n_wkeys), 1) - n_buf
    mask_w = (off <= t_row) & (off > t_row - WINDOW) & (off < dec_seq)
    p_w = _masked_softmax(_dot_nt(q, k) * SCALE, mask_w)
    o_w = _diag_heads(_dot(p_w.astype(BF16), v), NSA_KV_HEADS, rows // NSA_KV_HEADS)

    g = g_ref[0]
    o_ref[0] = (g[:, 0:1] * oc_ref[0] + g[:, 1:2] * o_s + g[:, 2:3] * o_w).astype(o_ref.dtype)


def block_to_key_matrix(n_keys):
    blk = np.arange(LANES)[:, None]
    key = np.arange(n_keys)[None, :]
    return jnp.asarray((blk == key // SLC_LEN).astype(np.float32)).astype(BF16)


def nsa_tail_sample(page_table, q_bd, k_pool, v_pool, k_new, v_new, kw, vw, kw_new, vw_new, o_c, sel, gates, dec_seq):
    b, rows, w = q_bd.shape
    n_pages = page_table.shape[1]
    n_keys = (n_pages + 1) * PAGE_SIZE
    pages = _page_specs(n_pages, PAGE_SIZE * NSA_KV_HEADS)
    e_mat = block_to_key_matrix(n_keys)
    per_b = lambda a: pl.BlockSpec((1, *a.shape[1:]), lambda bi, pt: (bi, 0, 0))
    grid_spec = pltpu.PrefetchScalarGridSpec(
        num_scalar_prefetch=1,
        grid=(b,),
        in_specs=[per_b(q_bd), *pages, *pages,
                  *[per_b(a) for a in (k_new, v_new, kw, vw, kw_new, vw_new, o_c, sel, gates)],
                  pl.BlockSpec(e_mat.shape, lambda bi, pt: (0, 0))],
        out_specs=pl.BlockSpec((1, rows, HEAD_DIM), lambda bi, pt: (bi, 0, 0)),
        scratch_shapes=[pltpu.VMEM((n_keys, w), BF16), pltpu.VMEM((n_keys, w), BF16)],
    )
    return pl.pallas_call(
        functools.partial(_nsa_tail_sample_kernel, n_pages=n_pages, dec_seq=dec_seq),
        grid_spec=grid_spec,
        out_shape=jax.ShapeDtypeStruct((b, rows, HEAD_DIM), BF16),
        compiler_params=_cparams(("parallel",)),
        name="nsa_tail_sample",
    )(page_table, q_bd, *([k_pool] * n_pages), *([v_pool] * n_pages), k_new, v_new, kw, vw, kw_new, vw_new,
      o_c, sel, gates, e_mat)


def prep_weights(w_in, w_branch_a, w_branch_b, w_out, w_ffn_gate, w_ffn_up, w_ffn_down, sb_w, nsa_w, kv_w, d_model):
    n_gate = 3 * (nsa_w // HEAD_DIM)
    widths = (sb_w, sb_w, sb_w, nsa_w, kv_w, kv_w, kv_w, kv_w, kv_w, kv_w, n_gate, d_model, d_model)
    offs = np.concatenate([[0], np.cumsum(widths)])
    n_main = int(offs[10])
    w_main = w_in[:, :n_main].astype(BF16)
    parts = [(w_main, int(offs[i]), int(widths[i])) for i in range(10)]
    parts += [w_in[:, int(offs[i]):int(offs[i + 1])].astype(BF16) for i in range(10, len(widths))]
    parts[10] = jnp.pad(parts[10], ((0, 0), (0, LANES - n_gate)))
    return dict(w_in=parts, wa=w_branch_a.astype(BF16), wb=w_branch_b.astype(BF16), w_out=w_out.astype(BF16),
                wg=w_ffn_gate.astype(BF16), wu=w_ffn_up.astype(BF16), wd=w_ffn_down.astype(BF16))


def projections(x2d, g_mix, w_parts, tables, table_rows):
    xn = rmsnorm(x2d, g_mix, BF16)
    rope = lambda w, dt: mm(xn, w, dt, epi="rope", extras=tables, table_rows=table_rows)
    plain = lambda w, dt: mm(xn, w, dt)
    q_sb, k_sb, v_sb = plain(w_parts[0], BF16), plain(w_parts[1], F32), plain(w_parts[2], F32)
    q_n = rope(w_parts[3], BF16)
    k_c, v_c = rope(w_parts[4], F32), plain(w_parts[5], F32)
    k_s, v_s = rope(w_parts[6], F32), plain(w_parts[7], F32)
    k_w, v_w = rope(w_parts[8], F32), plain(w_parts[9], F32)
    g_n = mm(xn, w_parts[10], F32, epi="sigmoid")
    g_a, g_b = plain(w_parts[11], F32), plain(w_parts[12], F32)
    return q_sb, k_sb, v_sb, q_n, k_c, v_c, k_s, v_s, k_w, v_w, g_n, g_a, g_b


def merge_and_ffn(x2d, o_sb, o_n, g_a, g_b, wts, g_ffn, g_final):
    mixed = merge(o_sb, o_n, wts["wa"], wts["wb"], g_a, g_b)
    h = mm(mixed, wts["w_out"], F32, epi="residual", extras=(x2d,))
    hn = rmsnorm(h, g_ffn, BF16)
    act = ffn_up(hn, wts["wg"], wts["wu"])
    y = mm(act, wts["wd"], F32, epi="residual", extras=(h,), tm=512, tn=256)
    return rmsnorm(y, g_final, F32)


def prompt_layer(x, wts, cw_k, cw_v, g_mix, g_ffn, g_final):
    b, t, d = x.shape
    x2d = x.reshape(b * t, d)
    tables = rope_tables(jnp.arange(t))
    (q_sb, k_sb, v_sb, q_n, k_c, v_c, k_s, v_s, k_w, v_w, g_n, g_a, g_b) = projections(
        x2d, g_mix, wts["w_in"], tables, t)
    r3 = lambda a: a.reshape(b, t, -1)
    n_sb_heads = q_sb.shape[1] // HEAD_DIM
    o_sb = sb_prompt(r3(q_sb), r3(k_sb), r3(v_sb), n_sb_heads)

    n_chunk = t // CMP_STRIDE
    ck = compress_prompt(r3(k_c), *cw_k).reshape(b, NSA_KV_HEADS, n_chunk, HEAD_DIM)
    cv = compress_prompt(r3(v_c), *cw_v).reshape(b, NSA_KV_HEADS, n_chunk, HEAD_DIM)
    n_gate = 3 * NSA_GROUP
    gates = g_n[:, :NSA_KV_HEADS * n_gate].reshape(b, t, NSA_KV_HEADS, n_gate).transpose(0, 2, 1, 3)
    gates = jnp.pad(gates, ((0, 0), (0, 0), (0, 0), (0, LANES - n_gate)))
    m_mat = cmp_to_slc_matrix(n_chunk, n_chunk - CMP_LEN // CMP_STRIDE + 1, t // SLC_LEN)
    o_n = nsa_prompt(r3(q_n), ck, cv, r3(k_s), r3(v_s), r3(k_w), r3(v_w), gates, m_mat)

    y = merge_and_ffn(x2d, o_sb.reshape(b * t, -1), o_n.reshape(b * t, -1), g_a, g_b, wts, g_ffn, g_final)
    heads = lambda a: a.reshape(1, b, t, -1, HEAD_DIM)
    keep = min(WINDOW, t)
    states = (heads(k_sb), heads(v_sb), heads(k_c), heads(v_c), heads(k_s), heads(v_s),
              heads(k_w)[:, :, t - keep:], heads(v_w)[:, :, t - keep:])
    return y.reshape(b, t, d), states


def _block_diag_rows(q, n_heads):
    b, r = q.shape[:2]
    qt = q.transpose(0, 2, 1, 3)
    eye = jnp.eye(n_heads, dtype=q.dtype)
    return (qt[:, :, :, None, :] * eye[None, :, None, :, None]).reshape(b, n_heads * r, n_heads * HEAD_DIM)


def _pad_new(a, b, t):
    return jnp.pad(a.reshape(b, t, -1).astype(BF16), ((0, 0), (0, PAGE_SIZE - t), (0, 0)))


def sample_layer(x, page_table, caches, wts, cw_k, cw_v, g_mix, g_ffn, g_final):
    c_sb_k, c_sb_v, c_cmp_k, c_cmp_v, c_slc_k, c_slc_v, w_buf_k, w_buf_v = caches
    b, t, d = x.shape
    n_pages = page_table.shape[1]
    past_len = n_pages * PAGE_SIZE
    x2d = x.reshape(b * t, d)
    tables = rope_tables(past_len + jnp.arange(b * t) % t)
    (q_sb, k_sb, v_sb, q_n, k_c, v_c, k_s, v_s, k_w, v_w, g_n, g_a, g_b) = projections(
        x2d, g_mix, wts["w_in"], tables, b * t)
    pool_rows = lambda a: a.reshape(a.shape[0], -1, HEAD_DIM)

    n_sb_heads = q_sb.shape[1] // HEAD_DIM
    q_bd = _block_diag_rows(q_sb.reshape(b, t, n_sb_heads, HEAD_DIM), n_sb_heads)
    o_sb = sb_sample(page_table, q_bd, pool_rows(c_sb_k), pool_rows(c_sb_v), _pad_new(k_sb, b, t),
                     _pad_new(v_sb, b, t), n_sb_heads, t)
    o_sb = o_sb.reshape(b, n_sb_heads, t, HEAD_DIM).transpose(0, 2, 1, 3).reshape(b * t, -1)

    qn5 = q_n.reshape(b, t, NSA_KV_HEADS, NSA_GROUP, HEAD_DIM).transpose(0, 2, 3, 1, 4)
    rows = NSA_KV_HEADS * NSA_GROUP * t
    q_rows = qn5.reshape(b, rows, HEAD_DIM)
    qn_bd = _block_diag_rows(qn5.reshape(b, NSA_KV_HEADS, NSA_GROUP * t, HEAD_DIM).transpose(0, 2, 1, 3), NSA_KV_HEADS)
    n_chunk = past_len // CMP_STRIDE
    n_slc = -(-(past_len + t) // SLC_LEN)
    m_mat = cmp_to_slc_matrix(n_chunk, n_chunk - CMP_LEN // CMP_STRIDE + 1, n_slc)
    o_c, sel = cmp_sample(page_table, q_rows, pool_rows(c_cmp_k), pool_rows(c_cmp_v), cw_k, cw_v, m_mat, t)
    gates = g_n[:, :rows // t * 3].reshape(b, t, NSA_KV_HEADS, NSA_GROUP, 3).transpose(0, 2, 3, 1, 4)
    gates = jnp.pad(gates.reshape(b, rows, 3), ((0, 0), (0, 0), (0, LANES - 3)))
    o_n = nsa_tail_sample(page_table, qn_bd, pool_rows(c_slc_k), pool_rows(c_slc_v), _pad_new(k_s, b, t),
                          _pad_new(v_s, b, t), pool_rows(w_buf_k), pool_rows(w_buf_v), _pad_new(k_w, b, t),
                          _pad_new(v_w, b, t), o_c, sel, gates, t)
    o_n = o_n.reshape(b, NSA_KV_HEADS, NSA_GROUP, t, HEAD_DIM).transpose(0, 3, 1, 2, 4).reshape(b * t, -1)

    y = merge_and_ffn(x2d, o_sb, o_n, g_a, g_b, wts, g_ffn, g_final)
    heads = lambda a: a.reshape(1, b, t, -1, HEAD_DIM)
    wk = jnp.concatenate([w_buf_k[None], heads(k_w)], axis=2)[:, :, t:]
    wv = jnp.concatenate([w_buf_v[None], heads(v_w)], axis=2)[:, :, t:]
    states = (heads(k_sb), heads(v_sb), heads(k_c), heads(v_c), heads(k_s), heads(v_s), wk, wv)
    return y.reshape(b, t, d), states


def kernel(x_prompt, x_sample, cache_sb_k, cache_sb_v, cache_cmp_k, cache_cmp_v, cache_slc_k, cache_slc_v,
           state_win_k, state_win_v, page_table, g_mix, w_in, w_cmp_k1, w_cmp_k2, pe_cmp_k, w_cmp_v1, w_cmp_v2,
           pe_cmp_v, w_branch_a, w_branch_b, w_out, g_ffn, w_ffn_gate, w_ffn_up, w_ffn_down, g_final):
    assert w_in.shape[0] == 1, "single-layer trunk"
    d_model = x_prompt.shape[-1]
    sb_w = cache_sb_k.shape[3] * HEAD_DIM
    kv_w = NSA_KV_HEADS * HEAD_DIM
    nsa_w = NSA_KV_HEADS * NSA_GROUP * HEAD_DIM
    wts = prep_weights(w_in[0], w_branch_a[0], w_branch_b[0], w_out[0], w_ffn_gate[0], w_ffn_up[0],
                       w_ffn_down[0], sb_w, nsa_w, kv_w, d_model)
    cw_k = prep_compress_weights(w_cmp_k1[0], pe_cmp_k[0], w_cmp_k2[0])
    cw_v = prep_compress_weights(w_cmp_v1[0], pe_cmp_v[0], w_cmp_v2[0])
    y_p, st_p = prompt_layer(x_prompt, wts, cw_k, cw_v, g_mix[0], g_ffn[0], g_final)
    caches = (cache_sb_k[0], cache_sb_v[0], cache_cmp_k[0], cache_cmp_v[0], cache_slc_k[0], cache_slc_v[0],
              state_win_k[0], state_win_v[0])
    y_s, st_s = sample_layer(x_sample, page_table, caches, wts, cw_k, cw_v, g_mix[0], g_ffn[0], g_final)
    return (y_p, y_s, *st_p, *st_s)
```
